```python
import math
import jax, jax.numpy as jnp
from jax import lax
import numpy as np

D_MODEL = 1024
BATCH = 8
SEQ = 4096
DEPTH = 4

HEAD_DIM = 64
N_HEADS = D_MODEL // HEAD_DIM
N_MIXERS = 3
DILATED_GROUPS = ((128, 1), (512, 4), (2048, 16))
N_DIL = len(DILATED_GROUPS)
GRID_W = 64
NA_ROWS = 8
NA_COLS = 16
SWA_RADIUS = 128
N_KV_HEADS = 4
GQA_GROUP = N_HEADS // N_KV_HEADS
D_FF = 256 * math.ceil(8 * D_MODEL / (3 * 256))
T5_BUCKETS = 32
T5_MAX_DISTANCE = 1024
N_A = (DEPTH + 2) // 3
N_B = (DEPTH + 1) // 3
N_C = DEPTH // 3
EPS = 1e-6
NEG_INF = -1e30

kernel_name = "hybrid_dilated_neighbourhood_swa_encoder"


def rmsnorm(x, g):
    xf = x.astype(jnp.float32)
    y = xf * lax.rsqrt(jnp.mean(xf * xf, axis=-1, keepdims=True) + EPS)
    return (y * g.astype(jnp.float32)).astype(x.dtype)


def t5_bucket(rel):
    half = T5_BUCKETS // 2
    max_exact = half // 2
    ret = jnp.where(rel > 0, half, 0)
    n = jnp.abs(rel)
    nf = jnp.maximum(n, 1).astype(jnp.float32)
    large = max_exact + (jnp.log(nf / max_exact) / math.log(T5_MAX_DISTANCE / max_exact)
                         * (half - max_exact)).astype(jnp.int32)
    large = jnp.minimum(large, half - 1)
    return ret + jnp.where(n < max_exact, n, large)


def t5_bias(table, rel):
    return jnp.moveaxis(table[t5_bucket(rel)], -1, 0)


def banded_attention(q, k, v, bias, sink):
    Q = bias.shape[-2]
    L = q.shape[0]
    nb = -(-L // Q)
    Lp = nb * Q
    pad = Lp - L
    scale = 1.0 / math.sqrt(q.shape[-1])
    qb = jnp.pad(q, ((0, pad),) + ((0, 0),) * (q.ndim - 1)).reshape(nb, Q, *q.shape[1:])

    def band(t):
        tp = jnp.pad(t, ((Q, pad + Q), (0, 0), (0, 0))).reshape(nb + 2, Q, *t.shape[1:])
        return jnp.concatenate([tp[:-2], tp[1:-1], tp[2:]], axis=1)

    kb, vb = band(k), band(v)
    kp = jnp.arange(-Q, Lp + Q).reshape(nb + 2, Q)
    key_pos = jnp.concatenate([kp[:-2], kp[1:-1], kp[2:]], axis=1)
    rel = jnp.arange(3 * Q)[None, :] - Q - jnp.arange(Q)[:, None]
    mask = (jnp.abs(rel) <= Q)[None] & ((key_pos >= 0) & (key_pos < L))[:, None, :]
    s = jnp.einsum('nqhgd,nkhd->nhgqk', qb, kb).astype(jnp.float32) * scale \
        + bias[None].astype(jnp.float32)
    s = jnp.where(mask[:, None, None], s, NEG_INF)
    lse = jax.nn.logsumexp(s, axis=-1)
    if sink is not None:
        lse = jnp.logaddexp(lse, sink.astype(jnp.float32)[None, :, :, None])
    p = jnp.exp(s - lse[..., None]).astype(v.dtype)
    o = jnp.einsum('nhgqk,nkhd->nqhgd', p, vb).reshape(Lp, *q.shape[1:])[:L]
    lse = lse.transpose(0, 3, 1, 2).reshape(Lp, q.shape[1], q.shape[2])[:L]
    return o, lse


def dilated_mixer(h, w_in, w_out, rel_bias):
    B_, S_, _ = h.shape
    qkv = (h @ w_in).reshape(B_, S_, N_DIL, 3, N_HEADS, HEAD_DIM)
    outs, lses = [], []
    for g, (window, d) in enumerate(DILATED_GROUPS):
        radius = window // (2 * d)
        L = S_ // d

        def by_residue(t):
            return t.reshape(B_, L, d, N_HEADS, HEAD_DIM).transpose(0, 2, 1, 3, 4) \
                    .reshape(B_ * d, L, N_HEADS, HEAD_DIM)

        q = by_residue(qkv[:, :, g, 0])[:, :, :, None]
        k = by_residue(qkv[:, :, g, 1])
        v = by_residue(qkv[:, :, g, 2])
        rel = (jnp.arange(3 * radius)[None, :] - radius - jnp.arange(radius)[:, None]) * d
        bias = t5_bias(rel_bias, rel)[:, None]
        o, lse = lax.map(lambda t: banded_attention(t[0], t[1], t[2], bias, None), (q, k, v))
        outs.append(o.reshape(B_, d, L, N_HEADS, HEAD_DIM).transpose(0, 2, 1, 3, 4)
                    .reshape(B_, S_, N_HEADS, HEAD_DIM))
        lses.append(lse.reshape(B_, d, L, N_HEADS).transpose(0, 2, 1, 3).reshape(B_, S_, N_HEADS))
    w = jax.nn.softmax(jnp.stack(lses), axis=0)
    o = jnp.einsum('gbsh,gbshd->bshd', w.astype(h.dtype), jnp.stack(outs))
    return o.reshape(B_, S_, D_MODEL) @ w_out


def neighbourhood_mixer(h, w_in, w_out, rpb):
    B_, S_, _ = h.shape
    rows = S_ // GRID_W
    kr = min(NA_ROWS, rows)
    kc = NA_COLS
    scale = 1.0 / math.sqrt(HEAD_DIM)
    qkv = (h @ w_in).reshape(B_, rows, GRID_W, 3, N_HEADS, HEAD_DIM)
    qg, kg, vg = qkv[:, :, :, 0], qkv[:, :, :, 1], qkv[:, :, :, 2]
    cols = jnp.arange(GRID_W)
    col_idx = jnp.clip(cols - kc // 2, 0, GRID_W - kc)[:, None] + jnp.arange(kc)[None, :]
    col_off = col_idx - cols[:, None] + (NA_COLS - 1)

    def row_fn(r):
        rs = jnp.clip(r - kr // 2, 0, rows - kr)
        ks = lax.dynamic_slice_in_dim(kg, rs, kr, axis=1)[:, :, col_idx]
        vs = lax.dynamic_slice_in_dim(vg, rs, kr, axis=1)[:, :, col_idx]
        qr = lax.dynamic_index_in_dim(qg, r, axis=1, keepdims=False)
        s = jnp.einsum('bqhd,brqkhd->bhqrk', qr, ks).astype(jnp.float32) * scale
        row_off = rs + jnp.arange(kr) - r + (NA_ROWS - 1)
        bias = rpb[:, row_off][:, :, col_off].transpose(0, 2, 1, 3)
        s = s + bias[None].astype(jnp.float32)
        p = jax.nn.softmax(s.reshape(B_, N_HEADS, GRID_W, kr * kc), axis=-1)
        p = p.reshape(B_, N_HEADS, GRID_W, kr, kc).astype(h.dtype)
        return jnp.einsum('bhqrk,brqkhd->bqhd', p, vs)

    o = lax.map(row_fn, jnp.arange(rows))
    o = o.transpose(1, 0, 2, 3, 4).reshape(B_, S_, D_MODEL)
    return o @ w_out


def window_gqa_mixer(h, w_in, w_out, sink, rel_bias):
    B_, S_, _ = h.shape
    qkv = h @ w_in
    nq, nk = N_HEADS * HEAD_DIM, N_KV_HEADS * HEAD_DIM
    q = qkv[..., :nq].reshape(B_, S_, N_KV_HEADS, GQA_GROUP, HEAD_DIM)
    k = qkv[..., nq:nq + nk].reshape(B_, S_, N_KV_HEADS, HEAD_DIM)
    v = qkv[..., nq + nk:].reshape(B_, S_, N_KV_HEADS, HEAD_DIM)
    R = SWA_RADIUS
    rel = jnp.arange(3 * R)[None, :] - R - jnp.arange(R)[:, None]
    bias = t5_bias(rel_bias, rel).reshape(N_KV_HEADS, GQA_GROUP, R, 3 * R)
    sk = sink.reshape(N_KV_HEADS, GQA_GROUP)
    o, _ = lax.map(lambda t: banded_attention(t[0], t[1], t[2], bias, sk), (q, k, v))
    return o.reshape(B_, S_, D_MODEL) @ w_out


def swiglu(h, w_in, w_out):
    gu = h @ w_in
    gate, up = gu[..., :D_FF], gu[..., D_FF:]
    return (jax.nn.silu(gate) * up) @ w_out


def setup_inputs(seed: int = 0) -> dict:
    key = jax.random.key(seed)
    ks = jax.random.split(key, 20)
    D = D_MODEL
    nrm = jax.random.normal
    f32 = jnp.float32
    return {
        "x": nrm(ks[0], (BATCH, SEQ, D), f32),
        "c": nrm(ks[1], (BATCH, D), f32),
        "rel_bias": 0.5 * nrm(ks[2], (T5_BUCKETS, N_HEADS), f32),
        "ada_w": 0.5 * D ** -0.5 * nrm(ks[3], (DEPTH, D, 6 * D), f32),
        "ada_b": 0.01 * nrm(ks[4], (DEPTH, 6 * D), f32),
        "norm_mix": 1.0 + 0.01 * nrm(ks[5], (DEPTH, D), f32),
        "norm_ffn": 1.0 + 0.01 * nrm(ks[6], (DEPTH, D), f32),
        "norm_final": 1.0 + 0.01 * nrm(ks[7], (D,), f32),
        "a_w_in": D ** -0.5 * nrm(ks[8], (N_A, D, N_DIL * 3 * N_HEADS * HEAD_DIM), f32),
        "a_w_out": D ** -0.5 * nrm(ks[9], (N_A, D, D), f32),
        "b_w_in": D ** -0.5 * nrm(ks[10], (N_B, D, 3 * D), f32),
        "b_w_out": D ** -0.5 * nrm(ks[11], (N_B, D, D), f32),
        "b_rpb": 0.5 * nrm(ks[12], (N_B, N_HEADS, 2 * NA_ROWS - 1, 2 * NA_COLS - 1), f32),
        "c_w_in": D ** -0.5 * nrm(ks[13], (N_C, D, (N_HEADS + 2 * N_KV_HEADS) * HEAD_DIM), f32),
        "c_w_out": D ** -0.5 * nrm(ks[14], (N_C, D, D), f32),
        "c_sink": 0.5 * nrm(ks[15], (N_C, N_HEADS), f32),
        "ffn_w_in": D ** -0.5 * nrm(ks[16], (DEPTH, D, 2 * D_FF), f32),
        "ffn_w_out": D_FF ** -0.5 * nrm(ks[17], (DEPTH, D_FF, D), f32),
    }


def reference(x, c, rel_bias, ada_w, ada_b, norm_mix, norm_ffn, norm_final,
              a_w_in, a_w_out, b_w_in, b_w_out, b_rpb,
              c_w_in, c_w_out, c_sink, ffn_w_in, ffn_w_out):
    cond = jax.nn.silu(c)
    for i in range(DEPTH):
        mod = (cond @ ada_w[i] + ada_b[i])[:, None, :]
        shift1, scale1, gate1, shift2, scale2, gate2 = jnp.split(mod, 6, axis=-1)
        h = rmsnorm(x, norm_mix[i]) * (1 + scale1) + shift1
        kind, j = i % N_MIXERS, i // N_MIXERS
        if kind == 0:
            m = dilated_mixer(h, a_w_in[j], a_w_out[j], rel_bias)
        elif kind == 1:
            m = neighbourhood_mixer(h, b_w_in[j], b_w_out[j], b_rpb[j])
        else:
            m = window_gqa_mixer(h, c_w_in[j], c_w_out[j], c_sink[j], rel_bias)
        x = x + gate1 * m
        h = rmsnorm(x, norm_ffn[i]) * (1 + scale2) + shift2
        x = x + gate2 * swiglu(h, ffn_w_in[i], ffn_w_out[i])
    return rmsnorm(x, norm_final)
```

```python
import functools
import math

import jax
import jax.numpy as jnp
from jax import lax
from jax.experimental import pallas as pl
from jax.experimental.pallas import tpu as pltpu

F32 = jnp.float32
BF16 = jnp.bfloat16

HEAD_DIM = 64
N_HEADS = 16
HEADS_PER_BLOCK = 4
LANE_BLOCK = HEADS_PER_BLOCK * HEAD_DIM
N_HEAD_BLOCKS = N_HEADS // HEADS_PER_BLOCK
DILATED_GROUPS = ((128, 1), (512, 4), (2048, 16))
GRID_W = 64
NA_ROWS = 8
NA_COLS = 16
SWA_RADIUS = 128
N_KV_HEADS = 4
T5_BUCKETS = 32
T5_MAX_DISTANCE = 1024
EPS = 1e-6
NEG_INF = -1e30
LSE_LANES = 128
Q_BLOCK = 128
VMEM_LIMIT = 56 * 1024 * 1024


def _cparams(n_axes, vmem=VMEM_LIMIT):
    return pltpu.CompilerParams(dimension_semantics=("arbitrary",) * n_axes, vmem_limit_bytes=vmem)


def _resident(shape):
    return pl.BlockSpec(shape, lambda *_: (0,) * len(shape), pipeline_mode=pl.Buffered(1))


def _ada_kernel(c_ref, w_ref, b_ref, o_ref):
    c = c_ref[...]
    cond = c * jax.nn.sigmoid(c)
    o_ref[0] = jnp.dot(cond.astype(BF16), w_ref[0].astype(BF16), preferred_element_type=F32) + b_ref[0]


def _ada(c, ada_w, ada_b, tn=1536):
    depth, d, n = ada_w.shape
    b = c.shape[0]
    return pl.pallas_call(
        _ada_kernel,
        grid=(depth, n // tn),
        in_specs=[pl.BlockSpec((b, d), lambda l, j: (0, 0)),
                  pl.BlockSpec((1, d, tn), lambda l, j: (l, 0, j)),
                  pl.BlockSpec((1, 1, tn), lambda l, j: (l, 0, j))],
        out_specs=pl.BlockSpec((1, b, tn), lambda l, j: (l, 0, j)),
        out_shape=jax.ShapeDtypeStruct((depth, b, n), F32),
        compiler_params=_cparams(2),
        name="ada_mod",
    )(c, ada_w, ada_b.reshape(depth, 1, n))


def _modulated_norm(x, g, mod, shift_row, scale_row):
    ms = jnp.mean(x * x, axis=-1, keepdims=True)
    y = x * lax.rsqrt(ms + EPS) * g
    return y * (1.0 + mod[scale_row:scale_row + 1]) + mod[shift_row:shift_row + 1]


def _proj_kernel(x_ref, g_ref, mod_ref, w_ref, o_ref, *, shift_row, scale_row, n_chunk):
    h = _modulated_norm(x_ref[...], g_ref[...], mod_ref[0], shift_row, scale_row).astype(BF16)
    n = w_ref.shape[1]
    for c0 in range(0, n, n_chunk):
        o_ref[:, c0:c0 + n_chunk] = jnp.dot(
            h, w_ref[:, c0:c0 + n_chunk], preferred_element_type=F32).astype(o_ref.dtype)


def _proj(x, g, mod, w, seq, *, tm=512, n_chunk=512):
    m, d = x.shape
    n = w.shape[1]
    per_seq = seq // tm
    return pl.pallas_call(
        functools.partial(_proj_kernel, shift_row=0, scale_row=1, n_chunk=n_chunk),
        grid=(m // tm,),
        in_specs=[pl.BlockSpec((tm, d), lambda i: (i, 0)),
                  _resident((1, d)),
                  pl.BlockSpec((1, 6, d), lambda i: (i // per_seq, 0, 0)),
                  _resident((d, n))],
        out_specs=pl.BlockSpec((tm, n), lambda i: (i, 0)),
        out_shape=jax.ShapeDtypeStruct((m, n), BF16),
        compiler_params=_cparams(1),
        name="norm_proj",
    )(x, g.reshape(1, d), mod, w)


def _outproj_kernel(o_ref, x_ref, mod_ref, w_ref, out_ref):
    m = jnp.dot(o_ref[...], w_ref[...], preferred_element_type=F32)
    out_ref[...] = x_ref[...] + mod_ref[0][2:3] * m


def _outproj(o, x, mod, w, seq, *, tm=512):
    m, d = x.shape
    per_seq = seq // tm
    return pl.pallas_call(
        _outproj_kernel,
        grid=(m // tm,),
        in_specs=[pl.BlockSpec((tm, d), lambda i: (i, 0)),
                  pl.BlockSpec((tm, d), lambda i: (i, 0)),
                  pl.BlockSpec((1, 6, d), lambda i: (i // per_seq, 0, 0)),
                  _resident((d, d))],
        out_specs=pl.BlockSpec((tm, d), lambda i: (i, 0)),
        out_shape=jax.ShapeDtypeStruct((m, d), F32),
        compiler_params=_cparams(1),
        name="out_proj",
    )(o, x, mod, w)


def _merge_outproj_kernel(o0_ref, o1_ref, o2_ref, l0_ref, l1_ref, l2_ref, e_ref, x_ref, mod_ref, w_ref, out_ref):
    l0, l1, l2 = l0_ref[...], l1_ref[...], l2_ref[...]
    mx = jnp.maximum(jnp.maximum(l0, l1), l2)
    e0, e1, e2 = jnp.exp(l0 - mx), jnp.exp(l1 - mx), jnp.exp(l2 - mx)
    inv = 1.0 / (e0 + e1 + e2)
    merged = None
    for e, o_ref in ((e0, o0_ref), (e1, o1_ref), (e2, o2_ref)):
        wgt = e * inv
        hi = wgt.astype(BF16)
        lo = (wgt - hi.astype(F32)).astype(BF16)
        wexp = jnp.dot(jnp.concatenate([hi, lo], axis=1), e_ref[...], preferred_element_type=F32)
        term = wexp * o_ref[...].astype(F32)
        merged = term if merged is None else merged + term
    m = jnp.dot(merged.astype(BF16), w_ref[...], preferred_element_type=F32)
    out_ref[...] = x_ref[...] + mod_ref[0][2:3] * m


def _merge_outproj(outs, lses, x, mod, w, seq, *, tm=512):
    m, d = x.shape
    per_seq = seq // tm
    head_of_lane = jnp.arange(d) // HEAD_DIM
    expand = (jnp.arange(LSE_LANES)[:, None] == head_of_lane[None, :]).astype(BF16)
    expand2 = jnp.concatenate([expand, expand], axis=0)
    row = lambda i: (i, 0)
    return pl.pallas_call(
        _merge_outproj_kernel,
        grid=(m // tm,),
        in_specs=[pl.BlockSpec((tm, d), row)] * 3 + [pl.BlockSpec((tm, LSE_LANES), row)] * 3 + [
            _resident((2 * LSE_LANES, d)),
            pl.BlockSpec((tm, d), row),
            pl.BlockSpec((1, 6, d), lambda i: (i // per_seq, 0, 0)),
            _resident((d, d))],
        out_specs=pl.BlockSpec((tm, d), row),
        out_shape=jax.ShapeDtypeStruct((m, d), F32),
        compiler_params=_cparams(1),
        name="merge_out_proj",
    )(*outs, *lses, expand2, x, mod, w)


def _ffn_kernel(x_ref, g_ref, mod_ref, win_ref, wout_ref, gf_ref, out_ref, *, d_ff, chunks, final):
    x = x_ref[...]
    mod = mod_ref[0]
    h = _modulated_norm(x, g_ref[...], mod, 3, 4).astype(BF16)
    acc = None
    for c0, c1 in chunks:
        gate = jnp.dot(h, win_ref[:, c0:c1], preferred_element_type=F32)
        up = jnp.dot(h, win_ref[:, d_ff + c0:d_ff + c1], preferred_element_type=F32)
        act = (gate * jax.nn.sigmoid(gate) * up).astype(BF16)
        part = jnp.dot(act, wout_ref[c0:c1, :], preferred_element_type=F32)
        acc = part if acc is None else acc + part
    y = x + mod[5:6] * acc
    if final:
        ms = jnp.mean(y * y, axis=-1, keepdims=True)
        y = y * lax.rsqrt(ms + EPS) * gf_ref[...]
    out_ref[...] = y


def _ffn(x, g, mod, w_in, w_out, g_final, seq, *, final, tm=512, chunk=1024):
    m, d = x.shape
    d_ff = w_out.shape[0]
    per_seq = seq // tm
    chunks = tuple((c0, min(c0 + chunk, d_ff)) for c0 in range(0, d_ff, chunk))
    return pl.pallas_call(
        functools.partial(_ffn_kernel, d_ff=d_ff, chunks=chunks, final=final),
        grid=(m // tm,),
        in_specs=[pl.BlockSpec((tm, d), lambda i: (i, 0)),
                  _resident((1, d)),
                  pl.BlockSpec((1, 6, d), lambda i: (i // per_seq, 0, 0)),
                  _resident((d, 2 * d_ff)),
                  _resident((d_ff, d)),
                  _resident((1, d))],
        out_specs=pl.BlockSpec((tm, d), lambda i: (i, 0)),
        out_shape=jax.ShapeDtypeStruct((m, d), F32),
        compiler_params=_cparams(1),
        name="ffn",
    )(x, g.reshape(1, d), mod, w_in, w_out, g_final.reshape(1, d))


def _t5_bucket(rel):
    half = T5_BUCKETS // 2
    max_exact = half // 2
    ret = jnp.where(rel > 0, half, 0)
    n = jnp.abs(rel)
    nf = jnp.maximum(n, 1).astype(F32)
    large = max_exact + (jnp.log(nf / max_exact) / math.log(T5_MAX_DISTANCE / max_exact)
                         * (half - max_exact)).astype(jnp.int32)
    large = jnp.minimum(large, half - 1)
    return ret + jnp.where(n < max_exact, n, large)


def _t5_table_kernel(tab_ref, idx_ref, o_ref, *, permute_heads):
    p = pl.program_id(1)
    if permute_heads:
        head = (p % HEADS_PER_BLOCK) * N_KV_HEADS + p // HEADS_PER_BLOCK
    else:
        head = p
    idx = idx_ref[0]
    acc = jnp.full(idx.shape, NEG_INF, F32)
    for b in range(T5_BUCKETS):
        acc = jnp.where(idx == b, tab_ref[b * N_HEADS + head], acc)
    o_ref[0, 0] = acc


def _t5_table(rel_bias, radius, dilation, width, permute_heads=False):
    offs = jnp.arange(3)[:, None, None] * radius
    rel = jnp.arange(width)[None, None, :] - offs - jnp.arange(Q_BLOCK)[None, :, None]
    idx = jnp.where(jnp.abs(rel) <= radius, _t5_bucket(rel * dilation), -1).astype(jnp.int32)
    return pl.pallas_call(
        functools.partial(_t5_table_kernel, permute_heads=permute_heads),
        grid=(3, N_HEADS),
        in_specs=[pl.BlockSpec(memory_space=pltpu.SMEM),
                  pl.BlockSpec((1, Q_BLOCK, width), lambda v, h: (v, 0, 0))],
        out_specs=pl.BlockSpec((1, 1, Q_BLOCK, width), lambda v, h: (v, h, 0, 0)),
        out_shape=jax.ShapeDtypeStruct((3, N_HEADS, Q_BLOCK, width), F32),
        compiler_params=_cparams(2),
        name="t5_table",
    )(rel_bias.reshape(-1), idx)


NA_ROW_OFFS = 2 * NA_ROWS - 1
NA_COL_OFFS = 2 * NA_COLS - 1


def _na_table_kernel(rpb_ref, o_ref):
    head, t = pl.program_id(0), pl.program_id(1)
    shape = (GRID_W, 2 * GRID_W)
    w = lax.broadcasted_iota(jnp.int32, shape, 0)
    lane = lax.broadcasted_iota(jnp.int32, shape, 1)
    c = lax.bitwise_and(lane, GRID_W - 1)
    half = lax.shift_right_logical(lane, int(math.log2(GRID_W)))
    col_off = c - w + (NA_COLS - 1)
    col_start = jnp.clip(w - NA_COLS // 2, 0, GRID_W - NA_COLS)
    valid = (c >= col_start) & (c < col_start + NA_COLS)
    acc = jnp.full(shape, NEG_INF, F32)
    for hv in range(2):
        sel_half = jnp.where(valid & (half == hv), col_off, -1)
        for cv in range(NA_COL_OFFS):
            val = rpb_ref[(head * NA_ROW_OFFS + t + hv) * NA_COL_OFFS + cv]
            acc = jnp.where(sel_half == cv, val, acc)
    o_ref[0, 0] = acc


def _na_table(rpb):
    n_t = NA_ROW_OFFS - 1
    return pl.pallas_call(
        _na_table_kernel,
        grid=(N_HEADS, n_t),
        in_specs=[pl.BlockSpec(memory_space=pltpu.SMEM)],
        out_specs=pl.BlockSpec((1, 1, GRID_W, 2 * GRID_W), lambda h, t: (h, t, 0, 0)),
        out_shape=jax.ShapeDtypeStruct((N_HEADS, n_t, GRID_W, 2 * GRID_W), F32),
        compiler_params=_cparams(2),
        name="na_table",
    )(rpb.reshape(-1))


def _head_of_lane(shape):
    return lax.shift_right_logical(lax.broadcasted_iota(jnp.int32, shape, 1), int(math.log2(HEAD_DIM)))


def _stack_heads(q):
    qf = q.astype(F32) * (1.0 / math.sqrt(HEAD_DIM))
    grp = _head_of_lane(qf.shape)
    return jnp.concatenate([jnp.where(grp == j, qf, 0.0) for j in range(HEADS_PER_BLOCK)], axis=0).astype(BF16)


def _softmax_heads(s, rows, bias_fn, sink_fn):
    ps, invs, ms, ls = [], [], [], []
    for j in range(HEADS_PER_BLOCK):
        sj = s[j * rows:(j + 1) * rows] + bias_fn(j)
        m = jnp.max(sj, axis=-1, keepdims=True)
        if sink_fn is not None:
            m = jnp.maximum(m, sink_fn(j))
        p = jnp.exp(sj - m)
        l = jnp.sum(p, axis=-1, keepdims=True)
        if sink_fn is not None:
            l = l + jnp.exp(sink_fn(j) - m)
        ps.append(p.astype(BF16))
        invs.append(1.0 / l)
        ms.append(m)
        ls.append(l)
    return jnp.concatenate(ps, axis=0), invs, ms, ls


def _unstack_heads(pv, invs, rows):
    grp = _head_of_lane((rows, LANE_BLOCK))
    out = pv[0:rows] * invs[0]
    for j in range(1, HEADS_PER_BLOCK):
        out = jnp.where(grp == j, pv[j * rows:(j + 1) * rows] * invs[j], out)
    return out


def _band_attn_kernel(*refs, n_blk, radius, width, has_sink, want_lse):
    if has_sink:
        sink_ref, refs = refs[0], refs[1:]
    q_ref, k_ref, v_ref, tab_ref, o_ref = refs[:5]
    lse_ref = refs[5] if want_lse else None
    seq = q_ref.shape[1]
    n_steps = seq // Q_BLOCK
    blk0 = pl.program_id(2) * n_blk

    if want_lse and n_blk < N_HEAD_BLOCKS:
        @pl.when(pl.program_id(2) == 0)
        def _():
            lse_ref[...] = jnp.zeros(lse_ref.shape, F32)

    def step(n, carry):
        q0 = pl.multiple_of(n * Q_BLOCK, Q_BLOCK)
        k0 = pl.multiple_of(jnp.clip(n * Q_BLOCK - radius, 0, seq - width), HEAD_DIM)
        variant = jnp.where(n == 0, 0, jnp.where(n == n_steps - 1, 2, 1))
        if want_lse:
            lane = lax.broadcasted_iota(jnp.int32, (Q_BLOCK, LSE_LANES), 1)
            if n_blk < N_HEAD_BLOCKS:
                lse_tile = lse_ref[0, pl.ds(q0, Q_BLOCK), :]
            else:
                lse_tile = jnp.zeros((Q_BLOCK, LSE_LANES), F32)
        for b in range(n_blk):
            lanes = slice(b * LANE_BLOCK, (b + 1) * LANE_BLOCK)
            klanes = lanes if k_ref.shape[2] > LANE_BLOCK else slice(0, LANE_BLOCK)
            q = q_ref[0, pl.ds(q0, Q_BLOCK), lanes]
            k = k_ref[0, pl.ds(k0, width), klanes]
            v = v_ref[0, pl.ds(k0, width), klanes]
            s = lax.dot_general(_stack_heads(q), k, (((1,), (1,)), ((), ())), preferred_element_type=F32)
            bias_fn = lambda j: tab_ref[variant, b * HEADS_PER_BLOCK + j]
            sink_fn = None
            if has_sink:
                sink_fn = lambda j: sink_ref[j * N_KV_HEADS + blk0 + b]
            p, invs, ms, ls = _softmax_heads(s, Q_BLOCK, bias_fn, sink_fn)
            pv = jnp.dot(p, v, preferred_element_type=F32)
            o_ref[0, pl.ds(q0, Q_BLOCK), lanes] = _unstack_heads(pv, invs, Q_BLOCK).astype(o_ref.dtype)
            if want_lse:
                for j in range(HEADS_PER_BLOCK):
                    col = (blk0 + b) * HEADS_PER_BLOCK + j
                    lse_tile = jnp.where(lane == col, ms[j] + jnp.log(ls[j]), lse_tile)
        if want_lse:
            lse_ref[0, pl.ds(q0, Q_BLOCK), :] = lse_tile
        return carry

    lax.fori_loop(0, n_steps, step, 0)


def _dilated_attention(qkv, table, group, dilation, batch, seq):
    d_model = N_HEADS * HEAD_DIM
    n_cols = qkv.shape[1]
    sub = seq // dilation
    n_blk = 1 if dilation == 1 else N_HEAD_BLOCKS
    wide = n_blk * LANE_BLOCK
    per_res = n_cols // wide
    base = group * 3 * d_model // wide
    stride = d_model // wide
    view = qkv.reshape(batch, sub, dilation * n_cols)

    def col(part):
        return lambda b, r, h: (b, 0, r * per_res + base + part * stride + h)

    o_spec = pl.BlockSpec((1, sub, wide), lambda b, r, h: (b, 0, r * stride + h))
    lse_spec = pl.BlockSpec((1, sub, LSE_LANES), lambda b, r, h: (b, 0, r))
    tab_spec = pl.BlockSpec((3, n_blk * HEADS_PER_BLOCK, Q_BLOCK, 2 * Q_BLOCK), lambda b, r, h: (0, h, 0, 0))
    o, lse = pl.pallas_call(
        functools.partial(_band_attn_kernel, n_blk=n_blk, radius=HEAD_DIM, width=2 * Q_BLOCK,
                          has_sink=False, want_lse=True),
        grid=(batch, dilation, N_HEAD_BLOCKS // n_blk),
        in_specs=[pl.BlockSpec((1, sub, wide), col(0)),
                  pl.BlockSpec((1, sub, wide), col(1)),
                  pl.BlockSpec((1, sub, wide), col(2)),
                  tab_spec],
        out_specs=[o_spec, lse_spec],
        out_shape=[jax.ShapeDtypeStruct((batch, sub, dilation * d_model), BF16),
                   jax.ShapeDtypeStruct((batch, sub, dilation * LSE_LANES), F32)],
        compiler_params=_cparams(3),
        name=f"dilated_attn_d{dilation}",
    )(view, view, view, table)
    return o.reshape(batch * seq, d_model), lse.reshape(batch * seq, LSE_LANES)


def _window_gqa_attention(qkv, table, sink, batch, seq):
    d_model = N_HEADS * HEAD_DIM
    view = qkv.reshape(batch, seq, qkv.shape[1])
    k_blk = d_model // LANE_BLOCK
    o = pl.pallas_call(
        functools.partial(_band_attn_kernel, n_blk=1, radius=SWA_RADIUS, width=3 * Q_BLOCK,
                          has_sink=True, want_lse=False),
        grid=(batch, 1, N_HEAD_BLOCKS),
        in_specs=[pl.BlockSpec(memory_space=pltpu.SMEM),
                  pl.BlockSpec((1, seq, LANE_BLOCK), lambda b, r, h: (b, 0, h)),
                  pl.BlockSpec((1, seq, LANE_BLOCK), lambda b, r, h: (b, 0, k_blk)),
                  pl.BlockSpec((1, seq, LANE_BLOCK), lambda b, r, h: (b, 0, k_blk + 1)),
                  pl.BlockSpec((3, HEADS_PER_BLOCK, Q_BLOCK, 3 * Q_BLOCK), lambda b, r, h: (0, h, 0, 0))],
        out_specs=pl.BlockSpec((1, seq, LANE_BLOCK), lambda b, r, h: (b, 0, h)),
        out_shape=jax.ShapeDtypeStruct((batch, seq, d_model), BF16),
        compiler_params=_cparams(3),
        name="window_gqa_attn",
    )(sink, view, view, view, table)
    return o.reshape(batch * seq, d_model)


def _nbr_attn_kernel(q_ref, k_ref, v_ref, tab_ref, o_ref):
    n_rows = q_ref.shape[1] // GRID_W
    n_keys = NA_ROWS * GRID_W

    def step(r, carry):
        row_start = jnp.clip(r - NA_ROWS // 2, 0, n_rows - NA_ROWS)
        t0 = row_start - r + (NA_ROWS - 1)
        q0 = pl.multiple_of(r * GRID_W, GRID_W)
        k0 = pl.multiple_of(row_start * GRID_W, GRID_W)
        q = q_ref[0, pl.ds(q0, GRID_W), :]
        k = k_ref[0, pl.ds(k0, n_keys), :]
        v = v_ref[0, pl.ds(k0, n_keys), :]
        s = lax.dot_general(_stack_heads(q), k, (((1,), (1,)), ((), ())), preferred_element_type=F32)
        bias_fn = lambda j: jnp.concatenate([tab_ref[j, t0 + 2 * pr] for pr in range(NA_ROWS // 2)], axis=1)
        p, invs, _, _ = _softmax_heads(s, GRID_W, bias_fn, None)
        pv = jnp.dot(p, v, preferred_element_type=F32)
        o_ref[0, pl.ds(q0, GRID_W), :] = _unstack_heads(pv, invs, GRID_W).astype(o_ref.dtype)
        return carry

    lax.fori_loop(0, n_rows, step, 0)


def _neighbourhood_attention(qkv, table, batch, seq):
    d_model = N_HEADS * HEAD_DIM
    view = qkv.reshape(batch, seq, 3 * d_model)
    nb = N_HEAD_BLOCKS
    blk = lambda part: pl.BlockSpec((1, seq, LANE_BLOCK), lambda b, h: (b, 0, part * nb + h))
    o = pl.pallas_call(
        _nbr_attn_kernel,
        grid=(batch, nb),
        in_specs=[blk(0), blk(1), blk(2),
                  pl.BlockSpec((HEADS_PER_BLOCK, NA_ROW_OFFS - 1, GRID_W, 2 * GRID_W), lambda b, h: (h, 0, 0, 0))],
        out_specs=pl.BlockSpec((1, seq, LANE_BLOCK), lambda b, h: (b, 0, h)),
        out_shape=jax.ShapeDtypeStruct((batch, seq, d_model), BF16),
        compiler_params=_cparams(2),
        name="neighbourhood_attn",
    )(view, view, view, table)
    return o.reshape(batch * seq, d_model)


def kernel(x, c, rel_bias, ada_w, ada_b, norm_mix, norm_ffn, norm_final, a_w_in, a_w_out, b_w_in, b_w_out, b_rpb, c_w_in, c_w_out, c_sink, ffn_w_in, ffn_w_out):
    batch, seq, d_model = x.shape
    depth = ada_w.shape[0]
    assert d_model == N_HEADS * HEAD_DIM and seq % (16 * 2 * Q_BLOCK) == 0 and seq // GRID_W >= NA_ROWS

    mods = _ada(c, ada_w, ada_b).reshape(depth, batch, 6, d_model)
    xf = x.reshape(batch * seq, d_model)

    tabs_a = [_t5_table(rel_bias, HEAD_DIM, d, 2 * Q_BLOCK) for _, d in DILATED_GROUPS]
    tab_c = _t5_table(rel_bias, SWA_RADIUS, 1, 3 * Q_BLOCK, permute_heads=True)
    perm = jnp.array([(p % HEADS_PER_BLOCK) * N_KV_HEADS + p // HEADS_PER_BLOCK for p in range(N_HEADS)])

    for i in range(depth):
        kind, j = i % 3, i // 3
        mod = mods[i]
        if kind == 0:
            qkv = _proj(xf, norm_mix[i], mod, a_w_in[j].astype(BF16), seq)
            outs, lses = [], []
            for g, (_, dil) in enumerate(DILATED_GROUPS):
                o, lse = _dilated_attention(qkv, tabs_a[g], g, dil, batch, seq)
                outs.append(o)
                lses.append(lse)
            xf = _merge_outproj(outs, lses, xf, mod, a_w_out[j].astype(BF16), seq)
        elif kind == 1:
            qkv = _proj(xf, norm_mix[i], mod, b_w_in[j].astype(BF16), seq)
            o = _neighbourhood_attention(qkv, _na_table(b_rpb[j]), batch, seq)
            xf = _outproj(o, xf, mod, b_w_out[j].astype(BF16), seq)
        else:
            nq = d_model
            w_q = c_w_in[j][:, :nq].reshape(d_model, N_HEADS, HEAD_DIM)[:, perm].reshape(d_model, nq)
            w_in = jnp.concatenate([w_q, c_w_in[j][:, nq:]], axis=1).astype(BF16)
            w_out = c_w_out[j].reshape(N_HEADS, HEAD_DIM, d_model)[perm].reshape(d_model, d_model).astype(BF16)
            qkv = _proj(xf, norm_mix[i], mod, w_in, seq)
            o = _window_gqa_attention(qkv, tab_c, c_sink[j], batch, seq)
            xf = _outproj(o, xf, mod, w_out, seq)
        xf = _ffn(xf, norm_ffn[i], mod, ffn_w_in[i].astype(BF16), ffn_w_out[i].astype(BF16), norm_final, seq,
                  final=(i == depth - 1))
    return xf.reshape(batch, seq, d_model)
```

```python
import functools
import math

import jax
import jax.numpy as jnp
from jax import lax
from jax.experimental import pallas as pl
from jax.experimental.pallas import tpu as pltpu

F32 = jnp.float32
BF16 = jnp.bfloat16

HEAD_DIM = 64
N_HEADS = 16
HEADS_PER_BLOCK = 4
LANE_BLOCK = HEADS_PER_BLOCK * HEAD_DIM
N_HEAD_BLOCKS = N_HEADS // HEADS_PER_BLOCK
DILATED_GROUPS = ((128, 1), (512, 4), (2048, 16))
GRID_W = 64
NA_ROWS = 8
NA_COLS = 16
SWA_RADIUS = 128
N_KV_HEADS = 4
T5_BUCKETS = 32
T5_MAX_DISTANCE = 1024
EPS = 1e-6
NEG_INF = -1e30
LOG2_E = math.log2(math.e)
LN_2 = math.log(2.0)
LANES = 128
LSE_LANES = LANES
Q_BLOCK = 128
ATTN_UNITS_IN_FLIGHT = 8
VMEM_LIMIT = 56 * 1024 * 1024


def _cparams(n_axes, vmem=VMEM_LIMIT):
    return pltpu.CompilerParams(dimension_semantics=("arbitrary",) * n_axes, vmem_limit_bytes=vmem)


def _resident(shape):
    return pl.BlockSpec(shape, lambda *_: (0,) * len(shape), pipeline_mode=pl.Buffered(1))


def _ada_kernel(c_ref, w_ref, b_ref, o_ref):
    c = c_ref[...]
    cond = c * jax.nn.sigmoid(c)
    o_ref[0] = jnp.dot(cond.astype(BF16), w_ref[0].astype(BF16), preferred_element_type=F32) + b_ref[0]


def _ada(c, ada_w, ada_b, tn=1536):
    depth, d, n = ada_w.shape
    b = c.shape[0]
    return pl.pallas_call(
        _ada_kernel,
        grid=(depth, n // tn),
        in_specs=[pl.BlockSpec((b, d), lambda l, j: (0, 0)),
                  pl.BlockSpec((1, d, tn), lambda l, j: (l, 0, j)),
                  pl.BlockSpec((1, 1, tn), lambda l, j: (l, 0, j))],
        out_specs=pl.BlockSpec((1, b, tn), lambda l, j: (l, 0, j)),
        out_shape=jax.ShapeDtypeStruct((depth, b, n), F32),
        compiler_params=_cparams(2),
        name="ada_mod",
    )(c, ada_w, ada_b.reshape(depth, 1, n))


def _modulated_norm(x, g, mod, shift_row, scale_row):
    ms = jnp.mean(x * x, axis=-1, keepdims=True)
    y = x * lax.rsqrt(ms + EPS) * g
    return y * (1.0 + mod[scale_row:scale_row + 1]) + mod[shift_row:shift_row + 1]


def _proj_kernel(x_ref, g_ref, mod_ref, w_ref, *rest, dilations, n_chunk):
    out_refs = rest[:len(dilations)]
    y = _modulated_norm(x_ref[...], g_ref[...], mod_ref[0], 0, 1)
    tm = y.shape[0]
    n_lane_tiles = y.shape[1] // LANES
    if max(dilations) > 1:
        hn_ref = rest[len(dilations)]
        for t in range(n_lane_tiles):
            hn_ref[t] = y[:, t * LANES:(t + 1) * LANES]
    n_group = w_ref.shape[1] // len(dilations)
    for grp, dil in enumerate(dilations):
        rows = tm // dil
        if dil == 1:
            h = y.astype(BF16)
        else:
            h = jnp.concatenate(
                [jnp.concatenate([hn_ref[t, pl.ds(r, rows, stride=dil), :] for t in range(n_lane_tiles)], axis=1)
                 for r in range(dil)], axis=0).astype(BF16)
        for c0 in range(0, n_group, n_chunk):
            w0 = grp * n_group + c0
            res = jnp.dot(h, w_ref[:, w0:w0 + n_chunk], preferred_element_type=F32).astype(BF16)
            for r in range(dil):
                out_refs[grp][0, r, :, c0:c0 + n_chunk] = res[r * rows:(r + 1) * rows]


def _proj(x, g, mod, w, batch, seq, dilations=(1,), *, tm=512, n_chunk=512):
    m, d = x.shape
    n_group = w.shape[1] // len(dilations)
    per_seq = seq // tm
    scratch = [pltpu.VMEM((d // LANES, tm, LANES), F32)] if max(dilations) > 1 else []
    return pl.pallas_call(
        functools.partial(_proj_kernel, dilations=dilations, n_chunk=n_chunk),
        grid=(m // tm,),
        in_specs=[pl.BlockSpec((tm, d), lambda i: (i, 0)),
                  _resident((1, d)),
                  pl.BlockSpec((1, 6, d), lambda i: (i // per_seq, 0, 0)),
                  _resident(w.shape)],
        out_specs=[pl.BlockSpec((1, dil, tm // dil, n_group), lambda i: (i // per_seq, 0, i % per_seq, 0))
                   for dil in dilations],
        out_shape=[jax.ShapeDtypeStruct((batch, dil, seq // dil, n_group), BF16) for dil in dilations],
        scratch_shapes=scratch,
        compiler_params=_cparams(1),
        name="norm_proj",
    )(x, g.reshape(1, d), mod, w)


def _outproj_kernel(o_ref, x_ref, mod_ref, w_ref, out_ref):
    m = jnp.dot(o_ref[...], w_ref[...], preferred_element_type=F32)
    out_ref[...] = x_ref[...] + mod_ref[0][2:3] * m


def _outproj(o, x, mod, w, seq, *, tm=512):
    m, d = x.shape
    per_seq = seq // tm
    return pl.pallas_call(
        _outproj_kernel,
        grid=(m // tm,),
        in_specs=[pl.BlockSpec((tm, d), lambda i: (i, 0)),
                  pl.BlockSpec((tm, d), lambda i: (i, 0)),
                  pl.BlockSpec((1, 6, d), lambda i: (i // per_seq, 0, 0)),
                  _resident((d, d))],
        out_specs=pl.BlockSpec((tm, d), lambda i: (i, 0)),
        out_shape=jax.ShapeDtypeStruct((m, d), F32),
        compiler_params=_cparams(1),
        name="out_proj",
    )(o, x, mod, w)


def _merge_outproj_kernel(*refs, dilations):
    n = len(dilations)
    o_refs, l_refs = refs[:n], refs[n:2 * n]
    e_ref, x_ref, mod_ref, w_ref, out_ref = refs[2 * n:2 * n + 5]
    scratch = refs[2 * n + 5:]
    tm = x_ref.shape[0]

    def natural_order(ref, dil, scr_ref):
        if dil == 1:
            return ref[0, 0].astype(F32)
        n_lane_tiles = scr_ref.shape[0]
        for r in range(dil):
            part = ref[0, r].astype(F32)
            for t in range(n_lane_tiles):
                scr_ref[t, pl.ds(r, tm // dil, stride=dil), :] = part[:, t * LANES:(t + 1) * LANES]
        return jnp.concatenate([scr_ref[t] for t in range(n_lane_tiles)], axis=1)

    outs, lses, k = [], [], 0
    for o_ref, l_ref, dil in zip(o_refs, l_refs, dilations):
        outs.append(natural_order(o_ref, dil, scratch[k] if dil > 1 else None))
        lses.append(natural_order(l_ref, dil, scratch[k + 1] if dil > 1 else None))
        k += 2 if dil > 1 else 0

    mx = functools.reduce(jnp.maximum, lses)
    es = [jnp.exp(l - mx) for l in lses]
    inv = 1.0 / functools.reduce(jnp.add, es)
    merged = None
    for e, o in zip(es, outs):
        wgt = e * inv
        hi = wgt.astype(BF16)
        lo = (wgt - hi.astype(F32)).astype(BF16)
        wexp = jnp.dot(jnp.concatenate([hi, lo], axis=1), e_ref[...], preferred_element_type=F32)
        term = wexp * o
        merged = term if merged is None else merged + term
    m = jnp.dot(merged.astype(BF16), w_ref[...], preferred_element_type=F32)
    out_ref[...] = x_ref[...] + mod_ref[0][2:3] * m


def _merge_outproj(outs, lses, dilations, x, mod, w, seq, *, tm=512):
    m, d = x.shape
    per_seq = seq // tm
    head_of_lane = jnp.arange(d) // HEAD_DIM
    expand = (jnp.arange(LSE_LANES)[:, None] == head_of_lane[None, :]).astype(BF16)
    expand2 = jnp.concatenate([expand, expand], axis=0)
    row = lambda i: (i, 0)
    grouped = lambda dil, cols: pl.BlockSpec((1, dil, tm // dil, cols), lambda i: (i // per_seq, 0, i % per_seq, 0))
    scratch = []
    for dil in dilations:
        if dil > 1:
            scratch += [pltpu.VMEM((d // LANES, tm, LANES), F32), pltpu.VMEM((1, tm, LSE_LANES), F32)]
    return pl.pallas_call(
        functools.partial(_merge_outproj_kernel, dilations=dilations),
        grid=(m // tm,),
        in_specs=[grouped(dil, d) for dil in dilations] + [grouped(dil, LSE_LANES) for dil in dilations] + [
            _resident((2 * LSE_LANES, d)),
            pl.BlockSpec((tm, d), row),
            pl.BlockSpec((1, 6, d), lambda i: (i // per_seq, 0, 0)),
            _resident((d, d))],
        out_specs=pl.BlockSpec((tm, d), row),
        out_shape=jax.ShapeDtypeStruct((m, d), F32),
        scratch_shapes=scratch,
        compiler_params=_cparams(1),
        name="merge_out_proj",
    )(*outs, *lses, expand2, x, mod, w)


def _ffn_kernel(x_ref, g_ref, mod_ref, win_ref, wout_ref, gf_ref, out_ref, *, d_ff, chunks, final):
    x = x_ref[...]
    mod = mod_ref[0]
    h = _modulated_norm(x, g_ref[...], mod, 3, 4).astype(BF16)
    acc = None
    for c0, c1 in chunks:
        gate = jnp.dot(h, win_ref[:, c0:c1], preferred_element_type=F32)
        up = jnp.dot(h, win_ref[:, d_ff + c0:d_ff + c1], preferred_element_type=F32)
        act = (gate * jax.nn.sigmoid(gate) * up).astype(BF16)
        part = jnp.dot(act, wout_ref[c0:c1, :], preferred_element_type=F32)
        acc = part if acc is None else acc + part
    y = x + mod[5:6] * acc
    if final:
        ms = jnp.mean(y * y, axis=-1, keepdims=True)
        y = y * lax.rsqrt(ms + EPS) * gf_ref[...]
    out_ref[...] = y


def _ffn(x, g, mod, w_in, w_out, g_final, seq, *, final, tm=512, chunk=1024):
    m, d = x.shape
    d_ff = w_out.shape[0]
    per_seq = seq // tm
    chunks = tuple((c0, min(c0 + chunk, d_ff)) for c0 in range(0, d_ff, chunk))
    return pl.pallas_call(
        functools.partial(_ffn_kernel, d_ff=d_ff, chunks=chunks, final=final),
        grid=(m // tm,),
        in_specs=[pl.BlockSpec((tm, d), lambda i: (i, 0)),
                  _resident((1, d)),
                  pl.BlockSpec((1, 6, d), lambda i: (i // per_seq, 0, 0)),
                  _resident((d, 2 * d_ff)),
                  _resident((d_ff, d)),
                  _resident((1, d))],
        out_specs=pl.BlockSpec((tm, d), lambda i: (i, 0)),
        out_shape=jax.ShapeDtypeStruct((m, d), F32),
        compiler_params=_cparams(1),
        name="ffn",
    )(x, g.reshape(1, d), mod, w_in, w_out, g_final.reshape(1, d))


def _t5_bucket(rel):
    half = T5_BUCKETS // 2
    max_exact = half // 2
    ret = jnp.where(rel > 0, half, 0)
    n = jnp.abs(rel)
    nf = jnp.maximum(n, 1).astype(F32)
    large = max_exact + (jnp.log(nf / max_exact) / math.log(T5_MAX_DISTANCE / max_exact)
                         * (half - max_exact)).astype(jnp.int32)
    large = jnp.minimum(large, half - 1)
    return ret + jnp.where(n < max_exact, n, large)


def _t5_table_kernel(tab_ref, idx_ref, o_ref, *, permute_heads):
    p = pl.program_id(1)
    if permute_heads:
        head = (p % HEADS_PER_BLOCK) * N_KV_HEADS + p // HEADS_PER_BLOCK
    else:
        head = p
    idx = idx_ref[0]
    acc = jnp.full(idx.shape, NEG_INF, F32)
    for b in range(T5_BUCKETS):
        acc = jnp.where(idx == b, tab_ref[b * N_HEADS + head] * LOG2_E, acc)
    o_ref[0, 0] = acc


def _t5_table(rel_bias, radius, dilation, width, permute_heads=False):
    offs = jnp.arange(3)[:, None, None] * radius
    rel = jnp.arange(width)[None, None, :] - offs - jnp.arange(Q_BLOCK)[None, :, None]
    idx = jnp.where(jnp.abs(rel) <= radius, _t5_bucket(rel * dilation), -1).astype(jnp.int32)
    return pl.pallas_call(
        functools.partial(_t5_table_kernel, permute_heads=permute_heads),
        grid=(3, N_HEADS),
        in_specs=[pl.BlockSpec(memory_space=pltpu.SMEM),
                  pl.BlockSpec((1, Q_BLOCK, width), lambda v, h: (v, 0, 0))],
        out_specs=pl.BlockSpec((1, 1, Q_BLOCK, width), lambda v, h: (v, h, 0, 0)),
        out_shape=jax.ShapeDtypeStruct((3, N_HEADS, Q_BLOCK, width), F32),
        compiler_params=_cparams(2),
        name="t5_table",
    )(rel_bias.reshape(-1), idx)


NA_ROW_OFFS = 2 * NA_ROWS - 1
NA_COL_OFFS = 2 * NA_COLS - 1


def _na_table_kernel(rpb_ref, o_ref):
    head, t = pl.program_id(0), pl.program_id(1)
    shape = (GRID_W, 2 * GRID_W)
    w = lax.broadcasted_iota(jnp.int32, shape, 0)
    lane = lax.broadcasted_iota(jnp.int32, shape, 1)
    c = lax.bitwise_and(lane, GRID_W - 1)
    half = lax.shift_right_logical(lane, int(math.log2(GRID_W)))
    col_off = c - w + (NA_COLS - 1)
    col_start = jnp.clip(w - NA_COLS // 2, 0, GRID_W - NA_COLS)
    valid = (c >= col_start) & (c < col_start + NA_COLS)
    acc = jnp.full(shape, NEG_INF, F32)
    for hv in range(2):
        sel_half = jnp.where(valid & (half == hv), col_off, -1)
        for cv in range(NA_COL_OFFS):
            val = rpb_ref[(head * NA_ROW_OFFS + t + hv) * NA_COL_OFFS + cv]
            acc = jnp.where(sel_half == cv, val * LOG2_E, acc)
    o_ref[0, 0] = acc


def _na_table(rpb):
    n_t = NA_ROW_OFFS - 1
    return pl.pallas_call(
        _na_table_kernel,
        grid=(N_HEADS, n_t),
        in_specs=[pl.BlockSpec(memory_space=pltpu.SMEM)],
        out_specs=pl.BlockSpec((1, 1, GRID_W, 2 * GRID_W), lambda h, t: (h, t, 0, 0)),
        out_shape=jax.ShapeDtypeStruct((N_HEADS, n_t, GRID_W, 2 * GRID_W), F32),
        compiler_params=_cparams(2),
        name="na_table",
    )(rpb.reshape(-1))


def _head_of_lane(shape):
    return lax.shift_right_logical(lax.broadcasted_iota(jnp.int32, shape, 1), int(math.log2(HEAD_DIM)))


def _stack_heads(q):
    qf = q.astype(F32) * (LOG2_E / math.sqrt(HEAD_DIM))
    grp = _head_of_lane(qf.shape)
    return jnp.concatenate([jnp.where(grp == j, qf, 0.0) for j in range(HEADS_PER_BLOCK)], axis=0).astype(BF16)


def _softmax_heads(s, rows, bias_fn, sink_fn):
    ps, invs, ms, ls = [], [], [], []
    for j in range(HEADS_PER_BLOCK):
        sj = s[j * rows:(j + 1) * rows] + bias_fn(j)
        m = jnp.max(sj, axis=-1, keepdims=True)
        if sink_fn is not None:
            m = jnp.maximum(m, sink_fn(j))
        p = jnp.exp2(sj - m)
        l = jnp.sum(p, axis=-1, keepdims=True)
        if sink_fn is not None:
            l = l + jnp.exp2(sink_fn(j) - m)
        ps.append(p.astype(BF16))
        invs.append(1.0 / l)
        ms.append(m)
        ls.append(l)
    return jnp.concatenate(ps, axis=0), invs, ms, ls


def _unstack_heads(pv, invs, rows):
    grp = _head_of_lane((rows, LANE_BLOCK))
    out = pv[0:rows] * invs[0]
    for j in range(1, HEADS_PER_BLOCK):
        out = jnp.where(grp == j, pv[j * rows:(j + 1) * rows] * invs[j], out)
    return out


def _band_attn_kernel(*refs, n_blk, radius, width, has_sink, want_lse, unroll):
    if has_sink:
        sink_ref, refs = refs[0], refs[1:]
    q_ref, k_ref, v_ref, tab_ref, o_ref = refs[:5]
    lse_ref = refs[5] if want_lse else None
    seq = q_ref.shape[1]
    n_steps = seq // Q_BLOCK
    blk0 = pl.program_id(2) * n_blk
    lse_accumulates = want_lse and n_blk < N_HEAD_BLOCKS

    if lse_accumulates:
        @pl.when(pl.program_id(2) == 0)
        def _():
            lse_ref[...] = jnp.zeros(lse_ref.shape, F32)

    def step(n, carry):
        q0 = pl.multiple_of(n * Q_BLOCK, Q_BLOCK)
        k0 = pl.multiple_of(jnp.clip(n * Q_BLOCK - radius, 0, seq - width), HEAD_DIM)
        variant = jnp.where(n == 0, 0, jnp.where(n == n_steps - 1, 2, 1))
        if want_lse:
            lane = lax.broadcasted_iota(jnp.int32, (Q_BLOCK, LSE_LANES), 1)
            m_tile = jnp.zeros((Q_BLOCK, LSE_LANES), F32)
            l_tile = jnp.ones((Q_BLOCK, LSE_LANES), F32)
        for b in range(n_blk):
            lanes = slice(b * LANE_BLOCK, (b + 1) * LANE_BLOCK)
            klanes = lanes if k_ref.shape[2] > LANE_BLOCK else slice(0, LANE_BLOCK)
            q = q_ref[0, pl.ds(q0, Q_BLOCK), lanes]
            k = k_ref[0, pl.ds(k0, width), klanes]
            v = v_ref[0, pl.ds(k0, width), klanes]
            s = lax.dot_general(_stack_heads(q), k, (((1,), (1,)), ((), ())), preferred_element_type=F32)
            bias_fn = lambda j: tab_ref[variant, b * HEADS_PER_BLOCK + j]
            sink_fn = None
            if has_sink:
                sink_fn = lambda j: sink_ref[j * N_KV_HEADS + blk0 + b] * LOG2_E
            p, invs, ms, ls = _softmax_heads(s, Q_BLOCK, bias_fn, sink_fn)
            pv = jnp.dot(p, v, preferred_element_type=F32)
            o_ref[0, pl.ds(q0, Q_BLOCK), lanes] = _unstack_heads(pv, invs, Q_BLOCK).astype(o_ref.dtype)
            if want_lse:
                for j in range(HEADS_PER_BLOCK):
                    col = (blk0 + b) * HEADS_PER_BLOCK + j
                    m_tile = jnp.where(lane == col, ms[j], m_tile)
                    l_tile = jnp.where(lane == col, ls[j], l_tile)
        if want_lse:
            lse_tile = m_tile * LN_2 + jnp.log(l_tile)
            if lse_accumulates:
                lse_tile = lse_tile + lse_ref[0, pl.ds(q0, Q_BLOCK), :]
            lse_ref[0, pl.ds(q0, Q_BLOCK), :] = lse_tile
        return carry

    lax.fori_loop(0, n_steps, step, 0, unroll=unroll)


def _dilated_attention(qkv, table, dilation):
    batch, _, sub, n_cols = qkv.shape
    d_model = n_cols // 3
    n_seq = batch * dilation
    n_blk = 1 if dilation == 1 else N_HEAD_BLOCKS
    wide = n_blk * LANE_BLOCK
    stride = d_model // wide
    view = qkv.reshape(n_seq, sub, n_cols)
    n_steps = sub // Q_BLOCK

    def col(part):
        return lambda s, _, h: (s, 0, part * stride + h)

    o, lse = pl.pallas_call(
        functools.partial(_band_attn_kernel, n_blk=n_blk, radius=HEAD_DIM, width=2 * Q_BLOCK,
                          has_sink=False, want_lse=True, unroll=min(n_steps, ATTN_UNITS_IN_FLIGHT // n_blk)),
        grid=(n_seq, 1, N_HEAD_BLOCKS // n_blk),
        in_specs=[pl.BlockSpec((1, sub, wide), col(0)),
                  pl.BlockSpec((1, sub, wide), col(1)),
                  pl.BlockSpec((1, sub, wide), col(2)),
                  pl.BlockSpec((3, n_blk * HEADS_PER_BLOCK, Q_BLOCK, 2 * Q_BLOCK), lambda s, _, h: (0, h, 0, 0))],
        out_specs=[pl.BlockSpec((1, sub, wide), lambda s, _, h: (s, 0, h)),
                   pl.BlockSpec((1, sub, LSE_LANES), lambda s, _, h: (s, 0, 0))],
        out_shape=[jax.ShapeDtypeStruct((n_seq, sub, d_model), BF16),
                   jax.ShapeDtypeStruct((n_seq, sub, LSE_LANES), F32)],
        compiler_params=_cparams(3),
        name=f"dilated_attn_d{dilation}",
    )(view, view, view, table)
    return (o.reshape(batch, dilation, sub, d_model), lse.reshape(batch, dilation, sub, LSE_LANES))


def _window_gqa_attention(qkv, table, sink, batch, seq):
    d_model = N_HEADS * HEAD_DIM
    view = qkv.reshape(batch, seq, qkv.shape[-1])
    k_blk = d_model // LANE_BLOCK
    o = pl.pallas_call(
        functools.partial(_band_attn_kernel, n_blk=1, radius=SWA_RADIUS, width=3 * Q_BLOCK,
                          has_sink=True, want_lse=False, unroll=ATTN_UNITS_IN_FLIGHT // 2),
        grid=(batch, 1, N_HEAD_BLOCKS),
        in_specs=[pl.BlockSpec(memory_space=pltpu.SMEM),
                  pl.BlockSpec((1, seq, LANE_BLOCK), lambda b, r, h: (b, 0, h)),
                  pl.BlockSpec((1, seq, LANE_BLOCK), lambda b, r, h: (b, 0, k_blk)),
                  pl.BlockSpec((1, seq, LANE_BLOCK), lambda b, r, h: (b, 0, k_blk + 1)),
                  pl.BlockSpec((3, HEADS_PER_BLOCK, Q_BLOCK, 3 * Q_BLOCK), lambda b, r, h: (0, h, 0, 0))],
        out_specs=pl.BlockSpec((1, seq, LANE_BLOCK), lambda b, r, h: (b, 0, h)),
        out_shape=jax.ShapeDtypeStruct((batch, seq, d_model), BF16),
        compiler_params=_cparams(3),
        name="window_gqa_attn",
    )(sink, view, view, view, table)
    return o.reshape(batch * seq, d_model)


def _nbr_attn_kernel(q_ref, k_ref, v_ref, tab_ref, o_ref):
    n_rows = q_ref.shape[1] // GRID_W
    n_keys = NA_ROWS * GRID_W

    def step(r, carry):
        row_start = jnp.clip(r - NA_ROWS // 2, 0, n_rows - NA_ROWS)
        t0 = row_start - r + (NA_ROWS - 1)
        q0 = pl.multiple_of(r * GRID_W, GRID_W)
        k0 = pl.multiple_of(row_start * GRID_W, GRID_W)
        q = q_ref[0, pl.ds(q0, GRID_W), :]
        k = k_ref[0, pl.ds(k0, n_keys), :]
        v = v_ref[0, pl.ds(k0, n_keys), :]
        s = lax.dot_general(_stack_heads(q), k, (((1,), (1,)), ((), ())), preferred_element_type=F32)
        bias_fn = lambda j: jnp.concatenate([tab_ref[j, t0 + 2 * pr] for pr in range(NA_ROWS // 2)], axis=1)
        p, invs, _, _ = _softmax_heads(s, GRID_W, bias_fn, None)
        pv = jnp.dot(p, v, preferred_element_type=F32)
        o_ref[0, pl.ds(q0, GRID_W), :] = _unstack_heads(pv, invs, GRID_W).astype(o_ref.dtype)
        return carry

    lax.fori_loop(0, n_rows, step, 0, unroll=ATTN_UNITS_IN_FLIGHT)


def _neighbourhood_attention(qkv, table, batch, seq):
    d_model = N_HEADS * HEAD_DIM
    view = qkv.reshape(batch, seq, 3 * d_model)
    nb = N_HEAD_BLOCKS
    blk = lambda part: pl.BlockSpec((1, seq, LANE_BLOCK), lambda b, h: (b, 0, part * nb + h))
    o = pl.pallas_call(
        _nbr_attn_kernel,
        grid=(batch, nb),
        in_specs=[blk(0), blk(1), blk(2),
                  pl.BlockSpec((HEADS_PER_BLOCK, NA_ROW_OFFS - 1, GRID_W, 2 * GRID_W), lambda b, h: (h, 0, 0, 0))],
        out_specs=pl.BlockSpec((1, seq, LANE_BLOCK), lambda b, h: (b, 0, h)),
        out_shape=jax.ShapeDtypeStruct((batch, seq, d_model), BF16),
        compiler_params=_cparams(2),
        name="neighbourhood_attn",
    )(view, view, view, table)
    return o.reshape(batch * seq, d_model)


def kernel(x, c, rel_bias, ada_w, ada_b, norm_mix, norm_ffn, norm_final, a_w_in, a_w_out, b_w_in, b_w_out, b_rpb, c_w_in, c_w_out, c_sink, ffn_w_in, ffn_w_out):
    batch, seq, d_model = x.shape
    depth = ada_w.shape[0]
    assert d_model == N_HEADS * HEAD_DIM and seq % (16 * 2 * Q_BLOCK) == 0 and seq // GRID_W >= NA_ROWS

    mods = _ada(c, ada_w, ada_b).reshape(depth, batch, 6, d_model)
    xf = x.reshape(batch * seq, d_model)

    tabs_a = [_t5_table(rel_bias, HEAD_DIM, d, 2 * Q_BLOCK) for _, d in DILATED_GROUPS]
    tab_c = _t5_table(rel_bias, SWA_RADIUS, 1, 3 * Q_BLOCK, permute_heads=True)
    perm = jnp.array([(p % HEADS_PER_BLOCK) * N_KV_HEADS + p // HEADS_PER_BLOCK for p in range(N_HEADS)])

    for i in range(depth):
        kind, j = i % 3, i // 3
        mod = mods[i]
        if kind == 0:
            dils = tuple(dil for _, dil in DILATED_GROUPS)
            qkvs = _proj(xf, norm_mix[i], mod, a_w_in[j].astype(BF16), batch, seq, dils)
            outs, lses = zip(*[_dilated_attention(qkv, tab, dil) for qkv, tab, dil in zip(qkvs, tabs_a, dils)])
            xf = _merge_outproj(outs, lses, dils, xf, mod, a_w_out[j].astype(BF16), seq)
        elif kind == 1:
            [qkv] = _proj(xf, norm_mix[i], mod, b_w_in[j].astype(BF16), batch, seq)
            o = _neighbourhood_attention(qkv, _na_table(b_rpb[j]), batch, seq)
            xf = _outproj(o, xf, mod, b_w_out[j].astype(BF16), seq)
        else:
            nq = d_model
            w_q = c_w_in[j][:, :nq].reshape(d_model, N_HEADS, HEAD_DIM)[:, perm].reshape(d_model, nq)
            w_in = jnp.concatenate([w_q, c_w_in[j][:, nq:]], axis=1).astype(BF16)
            w_out = c_w_out[j].reshape(N_HEADS, HEAD_DIM, d_model)[perm].reshape(d_model, d_model).astype(BF16)
            [qkv] = _proj(xf, norm_mix[i], mod, w_in, batch, seq)
            o = _window_gqa_attention(qkv, tab_c, c_sink[j], batch, seq)
            xf = _outproj(o, xf, mod, w_out, seq)
        xf = _ffn(xf, norm_ffn[i], mod, ffn_w_in[i].astype(BF16), ffn_w_out[i].astype(BF16), norm_final, seq,
                  final=(i == depth - 1))
    return xf.reshape(batch, seq, d_model)
```

```python
import functools
import math

import jax
import jax.numpy as jnp
from jax import lax
from jax.experimental import pallas as pl
from jax.experimental.pallas import tpu as pltpu

F32 = jnp.float32
BF16 = jnp.bfloat16

HEAD_DIM = 64
N_HEADS = 16
LANES = 128
HEADS_PER_BLOCK = 4
LANE_BLOCK = HEADS_PER_BLOCK * HEAD_DIM
N_HEAD_BLOCKS = N_HEADS // HEADS_PER_BLOCK
DILATED_GROUPS = ((128, 1), (512, 4), (2048, 16))
GRID_W = 64
NA_ROWS = 8
NA_COLS = 16
NA_ROW_OFFS = 2 * NA_ROWS - 1
NA_COL_OFFS = 2 * NA_COLS - 1
SWA_RADIUS = 128
N_KV_HEADS = 4
T5_BUCKETS = 32
T5_MAX_DISTANCE = 1024
EPS = 1e-6
NEG_INF = -1e30
LOG2_E = math.log2(math.e)
Q_SCALE = LOG2_E / math.sqrt(HEAD_DIM)
STAT_LANES = LANES
Q_BLOCK = 128
ATTN_UNITS_IN_FLIGHT = 8
VMEM_LIMIT = 58 * 1024 * 1024


def _cparams(n_axes, vmem=VMEM_LIMIT):
    return pltpu.CompilerParams(dimension_semantics=("arbitrary",) * n_axes, vmem_limit_bytes=vmem)


def _resident(shape):
    return pl.BlockSpec(shape, lambda *_: (0,) * len(shape), pipeline_mode=pl.Buffered(1))


def _ada_kernel(c_ref, w_ref, b_ref, o_ref):
    c = c_ref[...]
    cond = c * jax.nn.sigmoid(c)
    o_ref[0] = jnp.dot(cond.astype(BF16), w_ref[0].astype(BF16), preferred_element_type=F32) + b_ref[0]


def _ada(c, ada_w, ada_b, tn=1536):
    depth, d, n = ada_w.shape
    b = c.shape[0]
    return pl.pallas_call(
        _ada_kernel,
        grid=(depth, n // tn),
        in_specs=[pl.BlockSpec((b, d), lambda l, j: (0, 0)),
                  pl.BlockSpec((1, d, tn), lambda l, j: (l, 0, j)),
                  pl.BlockSpec((1, 1, tn), lambda l, j: (l, 0, j))],
        out_specs=pl.BlockSpec((1, b, tn), lambda l, j: (l, 0, j)),
        out_shape=jax.ShapeDtypeStruct((depth, b, n), F32),
        compiler_params=_cparams(2),
        name="ada_mod",
    )(c, ada_w, ada_b.reshape(depth, 1, n))


def _modulated_norm(x, g, mod, shift_row, scale_row):
    ms = jnp.mean(x * x, axis=-1, keepdims=True)
    y = x * lax.rsqrt(ms + EPS) * g
    return y * (1.0 + mod[scale_row:scale_row + 1]) + mod[shift_row:shift_row + 1]


def _proj_kernel(x_ref, g_ref, mod_ref, w_ref, cs_ref, *rest, dilations, n_chunk):
    out_refs = rest[:len(dilations)]
    y = _modulated_norm(x_ref[...], g_ref[...], mod_ref[0], 0, 1)
    tm = y.shape[0]
    n_lane_tiles = y.shape[1] // LANES
    if max(dilations) > 1:
        hn_ref = rest[len(dilations)]
        for t in range(n_lane_tiles):
            hn_ref[t] = y[:, t * LANES:(t + 1) * LANES]
    n_group = w_ref.shape[1] // len(dilations)
    for grp, dil in enumerate(dilations):
        rows = tm // dil
        if dil == 1:
            h = y.astype(BF16)
        else:
            h = jnp.concatenate(
                [jnp.concatenate([hn_ref[t, pl.ds(r, rows, stride=dil), :] for t in range(n_lane_tiles)], axis=1)
                 for r in range(dil)], axis=0).astype(BF16)
        for c0 in range(0, n_group, n_chunk):
            w0 = grp * n_group + c0
            res = jnp.dot(h, w_ref[:, w0:w0 + n_chunk], preferred_element_type=F32)
            res = (res * cs_ref[:, w0:w0 + n_chunk]).astype(BF16)
            for r in range(dil):
                out_refs[grp][0, r, :, c0:c0 + n_chunk] = res[r * rows:(r + 1) * rows]


def _proj(x, g, mod, w, col_scale, batch, seq, dilations=(1,), *, tm=512, n_chunk=512):
    m, d = x.shape
    n_group = w.shape[1] // len(dilations)
    per_seq = seq // tm
    scratch = [pltpu.VMEM((d // LANES, tm, LANES), F32)] if max(dilations) > 1 else []
    return pl.pallas_call(
        functools.partial(_proj_kernel, dilations=dilations, n_chunk=n_chunk),
        grid=(m // tm,),
        in_specs=[pl.BlockSpec((tm, d), lambda i: (i, 0)),
                  _resident((1, d)),
                  pl.BlockSpec((1, 6, d), lambda i: (i // per_seq, 0, 0)),
                  _resident(w.shape),
                  _resident((1, w.shape[1]))],
        out_specs=[pl.BlockSpec((1, dil, tm // dil, n_group), lambda i: (i // per_seq, 0, i % per_seq, 0))
                   for dil in dilations],
        out_shape=[jax.ShapeDtypeStruct((batch, dil, seq // dil, n_group), BF16) for dil in dilations],
        scratch_shapes=scratch,
        compiler_params=_cparams(1),
        name="norm_proj",
    )(x, g.reshape(1, d), mod, w, col_scale.reshape(1, -1))


def _q_col_scale(n_cols, q_ranges):
    scale = jnp.ones((n_cols,), F32)
    for c0, c1 in q_ranges:
        scale = scale.at[c0:c1].set(Q_SCALE)
    return scale


def _post_kernel(*refs, dilations, merged, final, d_ff, chunks):
    n = len(dilations)
    o_refs = refs[:n]
    m_refs = refs[n:2 * n] if merged else ()
    l_refs = refs[2 * n:3 * n] if merged else refs[n:2 * n]
    k = 3 * n if merged else 2 * n
    e_ref, x_ref, mod_ref, wo_ref, g_ref, win_ref, wout_ref, gf_ref, out_ref = refs[k:k + 9]
    scratch = list(refs[k + 9:])
    tm = x_ref.shape[0]

    def natural_order(ref, dil):
        if dil == 1:
            return ref[0, 0].astype(F32)
        scr_ref = scratch.pop(0)
        n_lane_tiles = scr_ref.shape[0]
        for r in range(dil):
            part = ref[0, r].astype(F32)
            for t in range(n_lane_tiles):
                scr_ref[t, pl.ds(r, tm // dil, stride=dil), :] = part[:, t * LANES:(t + 1) * LANES]
        return jnp.concatenate([scr_ref[t] for t in range(n_lane_tiles)], axis=1)

    def expand(wgt):
        hi = wgt.astype(BF16)
        lo = (wgt - hi.astype(F32)).astype(BF16)
        return jnp.dot(jnp.concatenate([hi, lo], axis=1), e_ref[...], preferred_element_type=F32)

    is_head = lax.broadcasted_iota(jnp.int32, (tm, STAT_LANES), 1) < N_HEADS
    outs = [natural_order(r, dil) for r, dil in zip(o_refs, dilations)]
    dens = [jnp.where(is_head, natural_order(r, dil), 1.0) for r, dil in zip(l_refs, dilations)]
    if merged:
        maxes = [natural_order(r, dil) for r, dil in zip(m_refs, dilations)]
        top = functools.reduce(jnp.maximum, maxes)
        es = [jnp.exp2(mg - top) for mg in maxes]
        inv = 1.0 / functools.reduce(jnp.add, [e * l for e, l in zip(es, dens)])
        att = functools.reduce(jnp.add, [expand(jnp.where(is_head, e * inv, 0.0)) * o for e, o in zip(es, outs)])
    else:
        att = expand(jnp.where(is_head, 1.0 / dens[0], 0.0)) * outs[0]

    mod = mod_ref[0]
    x = x_ref[...] + mod[2:3] * jnp.dot(att.astype(BF16), wo_ref[...], preferred_element_type=F32)

    h = _modulated_norm(x, g_ref[...], mod, 3, 4).astype(BF16)
    acc = None
    for c0, c1 in chunks:
        gate = jnp.dot(h, win_ref[:, c0:c1], preferred_element_type=F32)
        up = jnp.dot(h, win_ref[:, d_ff + c0:d_ff + c1], preferred_element_type=F32)
        act = (gate * jax.nn.sigmoid(gate) * up).astype(BF16)
        part = jnp.dot(act, wout_ref[c0:c1, :], preferred_element_type=F32)
        acc = part if acc is None else acc + part
    y = x + mod[5:6] * acc
    if final:
        ms = jnp.mean(y * y, axis=-1, keepdims=True)
        y = y * lax.rsqrt(ms + EPS) * gf_ref[...]
    out_ref[...] = y


def _post(outs, maxes, dens, dilations, x, mod, w_o, g, w_in, w_out, g_final, seq, *, final, tm=512, chunk=512):
    m, d = x.shape
    d_ff = w_out.shape[0]
    per_seq = seq // tm
    merged = maxes is not None
    head_of_lane = jnp.arange(d) // HEAD_DIM
    expand = (jnp.arange(STAT_LANES)[:, None] == head_of_lane[None, :]).astype(BF16)
    expand2 = jnp.concatenate([expand, expand], axis=0)
    chunks = tuple((c0, min(c0 + chunk, d_ff)) for c0 in range(0, d_ff, chunk))
    row = lambda i: (i, 0)
    grouped = lambda dil, cols: pl.BlockSpec((1, dil, tm // dil, cols), lambda i: (i // per_seq, 0, i % per_seq, 0))
    n_stats = 2 if merged else 1
    scratch = []
    for dil in dilations:
        if dil > 1:
            scratch.append(pltpu.VMEM((d // LANES, tm, LANES), F32))
    for _ in range(n_stats):
        for dil in dilations:
            if dil > 1:
                scratch.append(pltpu.VMEM((1, tm, STAT_LANES), F32))
    stats = (list(maxes) if merged else []) + list(dens)
    return pl.pallas_call(
        functools.partial(_post_kernel, dilations=dilations, merged=merged, final=final, d_ff=d_ff, chunks=chunks),
        grid=(m // tm,),
        in_specs=[grouped(dil, d) for dil in dilations]
                 + [grouped(dil, STAT_LANES) for _ in range(n_stats) for dil in dilations]
                 + [_resident((2 * STAT_LANES, d)),
                    pl.BlockSpec((tm, d), row),
                    pl.BlockSpec((1, 6, d), lambda i: (i // per_seq, 0, 0)),
                    _resident((d, d)),
                    _resident((1, d)),
                    _resident((d, 2 * d_ff)),
                    _resident((d_ff, d)),
                    _resident((1, d))],
        out_specs=pl.BlockSpec((tm, d), row),
        out_shape=jax.ShapeDtypeStruct((m, d), F32),
        scratch_shapes=scratch,
        compiler_params=_cparams(1),
        name="post_attn_ffn",
    )(*outs, *stats, expand2, x, mod, w_o, g.reshape(1, d), w_in, w_out, g_final.reshape(1, d))


def _t5_bucket(rel):
    half = T5_BUCKETS // 2
    max_exact = half // 2
    ret = jnp.where(rel > 0, half, 0)
    n = jnp.abs(rel)
    nf = jnp.maximum(n, 1).astype(F32)
    large = max_exact + (jnp.log(nf / max_exact) / math.log(T5_MAX_DISTANCE / max_exact)
                         * (half - max_exact)).astype(jnp.int32)
    large = jnp.minimum(large, half - 1)
    return ret + jnp.where(n < max_exact, n, large)


def _t5_table_kernel(tab_ref, idx_ref, o_ref, *, permute_heads):
    idx = idx_ref[0]
    hits = [idx == b for b in range(T5_BUCKETS)]
    for p in range(N_HEADS):
        head = (p % HEADS_PER_BLOCK) * N_KV_HEADS + p // HEADS_PER_BLOCK if permute_heads else p
        acc = jnp.full(idx.shape, NEG_INF, F32)
        for b in range(T5_BUCKETS):
            acc = jnp.where(hits[b], tab_ref[b * N_HEADS + head] * LOG2_E, acc)
        o_ref[0, p] = acc


def _t5_table(rel_bias, radius, dilation, width, permute_heads=False):
    offs = jnp.arange(3)[:, None, None] * radius
    rel = jnp.arange(width)[None, None, :] - offs - jnp.arange(Q_BLOCK)[None, :, None]
    idx = jnp.where(jnp.abs(rel) <= radius, _t5_bucket(rel * dilation), -1).astype(jnp.int32)
    return pl.pallas_call(
        functools.partial(_t5_table_kernel, permute_heads=permute_heads),
        grid=(3,),
        in_specs=[pl.BlockSpec(memory_space=pltpu.SMEM),
                  pl.BlockSpec((1, Q_BLOCK, width), lambda v: (v, 0, 0))],
        out_specs=pl.BlockSpec((1, N_HEADS, Q_BLOCK, width), lambda v: (v, 0, 0, 0)),
        out_shape=jax.ShapeDtypeStruct((3, N_HEADS, Q_BLOCK, width), F32),
        compiler_params=_cparams(1),
        name="t5_table",
    )(rel_bias.reshape(-1), idx)


def _na_table_kernel(rpb_ref, o_ref):
    head = pl.program_id(0)
    shape = (GRID_W, 2 * GRID_W)
    w = lax.broadcasted_iota(jnp.int32, shape, 0)
    lane = lax.broadcasted_iota(jnp.int32, shape, 1)
    c = lax.bitwise_and(lane, GRID_W - 1)
    half = lax.shift_right_logical(lane, int(math.log2(GRID_W)))
    col_start = jnp.clip(w - NA_COLS // 2, 0, GRID_W - NA_COLS)
    valid = (c >= col_start) & (c < col_start + NA_COLS)
    col_off = jnp.where(valid, c - w + (NA_COLS - 1), -1)
    hits = [[(col_off == cv) & (half == hv) for cv in range(NA_COL_OFFS)] for hv in range(2)]
    for t in range(NA_ROW_OFFS - 1):
        acc = jnp.full(shape, NEG_INF, F32)
        for hv in range(2):
            for cv in range(NA_COL_OFFS):
                val = rpb_ref[(head * NA_ROW_OFFS + t + hv) * NA_COL_OFFS + cv]
                acc = jnp.where(hits[hv][cv], val * LOG2_E, acc)
        o_ref[0, t] = acc


def _na_table(rpb):
    n_t = NA_ROW_OFFS - 1
    return pl.pallas_call(
        _na_table_kernel,
        grid=(N_HEADS,),
        in_specs=[pl.BlockSpec(memory_space=pltpu.SMEM)],
        out_specs=pl.BlockSpec((1, n_t, GRID_W, 2 * GRID_W), lambda h: (h, 0, 0, 0)),
        out_shape=jax.ShapeDtypeStruct((N_HEADS, n_t, GRID_W, 2 * GRID_W), F32),
        compiler_params=_cparams(1),
        name="na_table",
    )(rpb.reshape(-1))


def _attend(q, k, v, rows, bias_fn, sink_fn):
    head_of_lane = lax.shift_right_logical(lax.broadcasted_iota(jnp.int32, (rows, LANE_BLOCK), 1),
                                           int(math.log2(HEAD_DIM)))
    stacked = jnp.concatenate(
        [q * jnp.where(head_of_lane == j, 1.0, 0.0).astype(BF16) for j in range(HEADS_PER_BLOCK)], axis=0)
    s = lax.dot_general(stacked, k, (((1,), (1,)), ((), ())), preferred_element_type=F32)
    ps, maxes, dens = [], [], []
    for j in range(HEADS_PER_BLOCK):
        sj = s[j * rows:(j + 1) * rows] + bias_fn(j)
        m = jnp.max(sj, axis=-1, keepdims=True)
        if sink_fn is not None:
            m = jnp.maximum(m, sink_fn(j))
        p = jnp.exp2(sj - m)
        l = jnp.sum(p, axis=-1, keepdims=True)
        if sink_fn is not None:
            l = l + jnp.exp2(sink_fn(j) - m)
        ps.append(p.astype(BF16))
        maxes.append(m)
        dens.append(l)
    pv = jnp.dot(jnp.concatenate(ps, axis=0), v, preferred_element_type=F32)
    out = pv[0:rows]
    for j in range(1, HEADS_PER_BLOCK):
        out = jnp.where(head_of_lane == j, pv[j * rows:(j + 1) * rows], out)
    return out, maxes, dens


def _stat_tile(cols, tile):
    lane = lax.broadcasted_iota(jnp.int32, tile, 1)
    out = jnp.zeros(tile, F32)
    for col, val in cols:
        out = jnp.where(lane == col, val, out)
    return out


def _band_attn_kernel(*refs, n_blk, radius, width, has_sink, want_max, unroll):
    if has_sink:
        sink_ref, refs = refs[0], refs[1:]
    q_ref, k_ref, v_ref, tab_ref, o_ref = refs[:5]
    stat_refs = refs[5:]
    seq = q_ref.shape[1]
    n_steps = seq // Q_BLOCK
    blk0 = pl.program_id(2) * n_blk
    accumulate = n_blk < N_HEAD_BLOCKS

    if accumulate:
        @pl.when(pl.program_id(2) == 0)
        def _():
            for ref in stat_refs:
                ref[...] = jnp.zeros(ref.shape, F32)

    def step(n, carry):
        q0 = pl.multiple_of(n * Q_BLOCK, Q_BLOCK)
        k0 = pl.multiple_of(jnp.clip(n * Q_BLOCK - radius, 0, seq - width), HEAD_DIM)
        variant = jnp.where(n == 0, 0, jnp.where(n == n_steps - 1, 2, 1))
        max_cols, den_cols = [], []
        for b in range(n_blk):
            lanes = slice(b * LANE_BLOCK, (b + 1) * LANE_BLOCK)
            klanes = lanes if k_ref.shape[2] > LANE_BLOCK else slice(0, LANE_BLOCK)
            q = q_ref[0, pl.ds(q0, Q_BLOCK), lanes]
            k = k_ref[0, pl.ds(k0, width), klanes]
            v = v_ref[0, pl.ds(k0, width), klanes]
            bias_fn = lambda j: tab_ref[variant, b * HEADS_PER_BLOCK + j]
            sink_fn = None
            if has_sink:
                sink_fn = lambda j: sink_ref[j * N_KV_HEADS + blk0 + b] * LOG2_E
            out, maxes, dens = _attend(q, k, v, Q_BLOCK, bias_fn, sink_fn)
            o_ref[0, pl.ds(q0, Q_BLOCK), lanes] = out.astype(o_ref.dtype)
            for j in range(HEADS_PER_BLOCK):
                col = (blk0 + b) * HEADS_PER_BLOCK + j
                max_cols.append((col, maxes[j]))
                den_cols.append((col, dens[j]))
        for ref, cols in zip(stat_refs, ([max_cols] if want_max else []) + [den_cols]):
            tile = _stat_tile(cols, (Q_BLOCK, STAT_LANES))
            if accumulate:
                tile = tile + ref[0, pl.ds(q0, Q_BLOCK), :]
            ref[0, pl.ds(q0, Q_BLOCK), :] = tile
        return carry

    lax.fori_loop(0, n_steps, step, 0, unroll=unroll)


def _dilated_attention(qkv, table, dilation):
    batch, _, sub, n_cols = qkv.shape
    d_model = n_cols // 3
    n_seq = batch * dilation
    n_blk = 1 if dilation == 1 else N_HEAD_BLOCKS
    wide = n_blk * LANE_BLOCK
    stride = d_model // wide
    view = qkv.reshape(n_seq, sub, n_cols)
    n_steps = sub // Q_BLOCK

    def col(part):
        return lambda s, _, h: (s, 0, part * stride + h)

    stat_spec = pl.BlockSpec((1, sub, STAT_LANES), lambda s, _, h: (s, 0, 0))
    stat_shape = jax.ShapeDtypeStruct((n_seq, sub, STAT_LANES), F32)
    o, mx, den = pl.pallas_call(
        functools.partial(_band_attn_kernel, n_blk=n_blk, radius=HEAD_DIM, width=2 * Q_BLOCK,
                          has_sink=False, want_max=True, unroll=min(n_steps, ATTN_UNITS_IN_FLIGHT // n_blk)),
        grid=(n_seq, 1, N_HEAD_BLOCKS // n_blk),
        in_specs=[pl.BlockSpec((1, sub, wide), col(0)),
                  pl.BlockSpec((1, sub, wide), col(1)),
                  pl.BlockSpec((1, sub, wide), col(2)),
                  pl.BlockSpec((3, n_blk * HEADS_PER_BLOCK, Q_BLOCK, 2 * Q_BLOCK), lambda s, _, h: (0, h, 0, 0))],
        out_specs=[pl.BlockSpec((1, sub, wide), lambda s, _, h: (s, 0, h)), stat_spec, stat_spec],
        out_shape=[jax.ShapeDtypeStruct((n_seq, sub, d_model), BF16), stat_shape, stat_shape],
        compiler_params=_cparams(3),
        name=f"dilated_attn_d{dilation}",
    )(view, view, view, table)
    grouped = lambda a: a.reshape(batch, dilation, sub, a.shape[-1])
    return grouped(o), grouped(mx), grouped(den)


def _window_gqa_attention(qkv, table, sink, batch, seq):
    d_model = N_HEADS * HEAD_DIM
    view = qkv.reshape(batch, seq, qkv.shape[-1])
    k_blk = d_model // LANE_BLOCK
    o, den = pl.pallas_call(
        functools.partial(_band_attn_kernel, n_blk=1, radius=SWA_RADIUS, width=3 * Q_BLOCK,
                          has_sink=True, want_max=False, unroll=ATTN_UNITS_IN_FLIGHT // 2),
        grid=(batch, 1, N_HEAD_BLOCKS),
        in_specs=[pl.BlockSpec(memory_space=pltpu.SMEM),
                  pl.BlockSpec((1, seq, LANE_BLOCK), lambda b, r, h: (b, 0, h)),
                  pl.BlockSpec((1, seq, LANE_BLOCK), lambda b, r, h: (b, 0, k_blk)),
                  pl.BlockSpec((1, seq, LANE_BLOCK), lambda b, r, h: (b, 0, k_blk + 1)),
                  pl.BlockSpec((3, HEADS_PER_BLOCK, Q_BLOCK, 3 * Q_BLOCK), lambda b, r, h: (0, h, 0, 0))],
        out_specs=[pl.BlockSpec((1, seq, LANE_BLOCK), lambda b, r, h: (b, 0, h)),
                   pl.BlockSpec((1, seq, STAT_LANES), lambda b, r, h: (b, 0, 0))],
        out_shape=[jax.ShapeDtypeStruct((batch, seq, d_model), BF16),
                   jax.ShapeDtypeStruct((batch, seq, STAT_LANES), F32)],
        compiler_params=_cparams(3),
        name="window_gqa_attn",
    )(sink, view, view, view, table)
    return o.reshape(batch, 1, seq, d_model), den.reshape(batch, 1, seq, STAT_LANES)


def _nbr_attn_kernel(q_ref, k_ref, v_ref, tab_ref, o_ref, den_ref):
    n_rows = q_ref.shape[1] // GRID_W
    n_keys = NA_ROWS * GRID_W
    blk = pl.program_id(1)

    @pl.when(blk == 0)
    def _():
        den_ref[...] = jnp.zeros(den_ref.shape, F32)

    def step(r, carry):
        row_start = jnp.clip(r - NA_ROWS // 2, 0, n_rows - NA_ROWS)
        t0 = row_start - r + (NA_ROWS - 1)
        q0 = pl.multiple_of(r * GRID_W, GRID_W)
        k0 = pl.multiple_of(row_start * GRID_W, GRID_W)
        q = q_ref[0, pl.ds(q0, GRID_W), :]
        k = k_ref[0, pl.ds(k0, n_keys), :]
        v = v_ref[0, pl.ds(k0, n_keys), :]
        bias_fn = lambda j: jnp.concatenate([tab_ref[j, t0 + 2 * pr] for pr in range(NA_ROWS // 2)], axis=1)
        out, _, dens = _attend(q, k, v, GRID_W, bias_fn, None)
        o_ref[0, pl.ds(q0, GRID_W), :] = out.astype(o_ref.dtype)
        cols = [(blk * HEADS_PER_BLOCK + j, dens[j]) for j in range(HEADS_PER_BLOCK)]
        den_ref[0, pl.ds(q0, GRID_W), :] = den_ref[0, pl.ds(q0, GRID_W), :] + _stat_tile(cols, (GRID_W, STAT_LANES))
        return carry

    lax.fori_loop(0, n_rows, step, 0, unroll=ATTN_UNITS_IN_FLIGHT)


def _neighbourhood_attention(qkv, table, batch, seq):
    d_model = N_HEADS * HEAD_DIM
    view = qkv.reshape(batch, seq, 3 * d_model)
    nb = N_HEAD_BLOCKS
    blk = lambda part: pl.BlockSpec((1, seq, LANE_BLOCK), lambda b, h: (b, 0, part * nb + h))
    o, den = pl.pallas_call(
        _nbr_attn_kernel,
        grid=(batch, nb),
        in_specs=[blk(0), blk(1), blk(2),
                  pl.BlockSpec((HEADS_PER_BLOCK, NA_ROW_OFFS - 1, GRID_W, 2 * GRID_W), lambda b, h: (h, 0, 0, 0))],
        out_specs=[pl.BlockSpec((1, seq, LANE_BLOCK), lambda b, h: (b, 0, h)),
                   pl.BlockSpec((1, seq, STAT_LANES), lambda b, h: (b, 0, 0))],
        out_shape=[jax.ShapeDtypeStruct((batch, seq, d_model), BF16),
                   jax.ShapeDtypeStruct((batch, seq, STAT_LANES), F32)],
        compiler_params=_cparams(2),
        name="neighbourhood_attn",
    )(view, view, view, table)
    return o.reshape(batch, 1, seq, d_model), den.reshape(batch, 1, seq, STAT_LANES)


def kernel(x, c, rel_bias, ada_w, ada_b, norm_mix, norm_ffn, norm_final, a_w_in, a_w_out, b_w_in, b_w_out, b_rpb, c_w_in, c_w_out, c_sink, ffn_w_in, ffn_w_out):
    batch, seq, d_model = x.shape
    depth = ada_w.shape[0]
    assert d_model == N_HEADS * HEAD_DIM and seq % (16 * 2 * Q_BLOCK) == 0 and seq // GRID_W >= NA_ROWS

    mods = _ada(c, ada_w, ada_b).reshape(depth, batch, 6, d_model)
    xf = x.reshape(batch * seq, d_model)

    dils = tuple(dil for _, dil in DILATED_GROUPS)
    tabs_a = [_t5_table(rel_bias, HEAD_DIM, dil, 2 * Q_BLOCK) for dil in dils]
    tab_c = _t5_table(rel_bias, SWA_RADIUS, 1, 3 * Q_BLOCK, permute_heads=True)
    perm = jnp.array([(p % HEADS_PER_BLOCK) * N_KV_HEADS + p // HEADS_PER_BLOCK for p in range(N_HEADS)])

    for i in range(depth):
        kind, j = i % 3, i // 3
        mod = mods[i]
        if kind == 0:
            n_in = a_w_in.shape[2]
            scale = _q_col_scale(n_in, [(g * 3 * d_model, g * 3 * d_model + d_model) for g in range(len(dils))])
            qkvs = _proj(xf, norm_mix[i], mod, a_w_in[j].astype(BF16), scale, batch, seq, dils)
            outs, maxes, dens = zip(*[_dilated_attention(qkv, tab, dil) for qkv, tab, dil in zip(qkvs, tabs_a, dils)])
            post_dils, w_o = dils, a_w_out[j].astype(BF16)
        elif kind == 1:
            scale = _q_col_scale(3 * d_model, [(0, d_model)])
            [qkv] = _proj(xf, norm_mix[i], mod, b_w_in[j].astype(BF16), scale, batch, seq)
            o, den = _neighbourhood_attention(qkv, _na_table(b_rpb[j]), batch, seq)
            outs, maxes, dens, post_dils, w_o = [o], None, [den], (1,), b_w_out[j].astype(BF16)
        else:
            w_q = c_w_in[j][:, :d_model].reshape(d_model, N_HEADS, HEAD_DIM)[:, perm].reshape(d_model, d_model)
            w_in = jnp.concatenate([w_q, c_w_in[j][:, d_model:]], axis=1).astype(BF16)
            w_o = c_w_out[j].reshape(N_HEADS, HEAD_DIM, d_model)[perm].reshape(d_model, d_model).astype(BF16)
            scale = _q_col_scale(w_in.shape[1], [(0, d_model)])
            [qkv] = _proj(xf, norm_mix[i], mod, w_in, scale, batch, seq)
            o, den = _window_gqa_attention(qkv, tab_c, c_sink[j], batch, seq)
            outs, maxes, dens, post_dils = [o], None, [den], (1,)
        xf = _post(outs, maxes, dens, post_dils, xf, mod, w_o, norm_ffn[i], ffn_w_in[i].astype(BF16),
                   ffn_w_out[i].astype(BF16), norm_final, seq, final=(i == depth - 1))
    return xf.reshape(batch, seq, d_model)
```

```python
import functools
import math

import jax
import jax.numpy as jnp
from jax import lax
from jax.experimental import pallas as pl
from jax.experimental.pallas import tpu as pltpu

F32 = jnp.float32
BF16 = jnp.bfloat16

HEAD_DIM = 64
N_HEADS = 16
LANES = 128
HEADS_PER_BLOCK = 4
LANE_BLOCK = HEADS_PER_BLOCK * HEAD_DIM
N_HEAD_BLOCKS = N_HEADS // HEADS_PER_BLOCK
DILATED_GROUPS = ((128, 1), (512, 4), (2048, 16))
GRID_W = 64
NA_ROWS = 8
NA_COLS = 16
NA_ROW_OFFS = 2 * NA_ROWS - 1
NA_COL_OFFS = 2 * NA_COLS - 1
SWA_RADIUS = 128
N_KV_HEADS = 4
T5_BUCKETS = 32
T5_MAX_DISTANCE = 1024
EPS = 1e-6
NEG_INF = -1e30
LOG2_E = math.log2(math.e)
Q_SCALE = LOG2_E / math.sqrt(HEAD_DIM)
STAT_LANES = LANES
Q_BLOCK = 128
ATTN_UNITS_IN_FLIGHT = 32
VMEM_LIMIT = 58 * 1024 * 1024


def _cparams(n_axes, vmem=VMEM_LIMIT):
    return pltpu.CompilerParams(dimension_semantics=("arbitrary",) * n_axes, vmem_limit_bytes=vmem)


def _resident(shape):
    return pl.BlockSpec(shape, lambda *_: (0,) * len(shape), pipeline_mode=pl.Buffered(1))


def _ada_kernel(c_ref, w_ref, b_ref, o_ref):
    c = c_ref[...]
    cond = c * jax.nn.sigmoid(c)
    o_ref[0] = jnp.dot(cond.astype(BF16), w_ref[0].astype(BF16), preferred_element_type=F32) + b_ref[0]


def _ada(c, ada_w, ada_b, tn=1536):
    depth, d, n = ada_w.shape
    b = c.shape[0]
    return pl.pallas_call(
        _ada_kernel,
        grid=(depth, n // tn),
        in_specs=[pl.BlockSpec((b, d), lambda l, j: (0, 0)),
                  pl.BlockSpec((1, d, tn), lambda l, j: (l, 0, j)),
                  pl.BlockSpec((1, 1, tn), lambda l, j: (l, 0, j))],
        out_specs=pl.BlockSpec((1, b, tn), lambda l, j: (l, 0, j)),
        out_shape=jax.ShapeDtypeStruct((depth, b, n), F32),
        compiler_params=_cparams(2),
        name="ada_mod",
    )(c, ada_w, ada_b.reshape(depth, 1, n))


def _modulated_norm(x, g, mod, shift_row, scale_row):
    ms = jnp.mean(x * x, axis=-1, keepdims=True)
    y = x * lax.rsqrt(ms + EPS) * g
    return y * (1.0 + mod[scale_row:scale_row + 1]) + mod[shift_row:shift_row + 1]


def _proj_kernel(x_ref, g_ref, mod_ref, w_ref, cs_ref, *rest, dilations, n_chunk):
    out_refs = rest[:len(dilations)]
    y = _modulated_norm(x_ref[...], g_ref[...], mod_ref[0], 0, 1)
    tm = y.shape[0]
    n_lane_tiles = y.shape[1] // LANES
    if max(dilations) > 1:
        hn_ref = rest[len(dilations)]
        for t in range(n_lane_tiles):
            hn_ref[t] = y[:, t * LANES:(t + 1) * LANES]
    n_group = w_ref.shape[1] // len(dilations)
    for grp, dil in enumerate(dilations):
        rows = tm // dil
        if dil == 1:
            h = y.astype(BF16)
        else:
            h = jnp.concatenate(
                [jnp.concatenate([hn_ref[t, pl.ds(r, rows, stride=dil), :] for t in range(n_lane_tiles)], axis=1)
                 for r in range(dil)], axis=0).astype(BF16)
        for c0 in range(0, n_group, n_chunk):
            w0 = grp * n_group + c0
            res = jnp.dot(h, w_ref[:, w0:w0 + n_chunk], preferred_element_type=F32)
            res = (res * cs_ref[:, w0:w0 + n_chunk]).astype(BF16)
            for r in range(dil):
                out_refs[grp][0, r, :, c0:c0 + n_chunk] = res[r * rows:(r + 1) * rows]


def _proj(x, g, mod, w, col_scale, batch, seq, dilations=(1,), *, tm=512, n_chunk=512):
    m, d = x.shape
    n_group = w.shape[1] // len(dilations)
    per_seq = seq // tm
    scratch = [pltpu.VMEM((d // LANES, tm, LANES), F32)] if max(dilations) > 1 else []
    return pl.pallas_call(
        functools.partial(_proj_kernel, dilations=dilations, n_chunk=n_chunk),
        grid=(m // tm,),
        in_specs=[pl.BlockSpec((tm, d), lambda i: (i, 0)),
                  _resident((1, d)),
                  pl.BlockSpec((1, 6, d), lambda i: (i // per_seq, 0, 0)),
                  _resident(w.shape),
                  _resident((1, w.shape[1]))],
        out_specs=[pl.BlockSpec((1, dil, tm // dil, n_group), lambda i: (i // per_seq, 0, i % per_seq, 0))
                   for dil in dilations],
        out_shape=[jax.ShapeDtypeStruct((batch, dil, seq // dil, n_group), BF16) for dil in dilations],
        scratch_shapes=scratch,
        compiler_params=_cparams(1),
        name="norm_proj",
    )(x, g.reshape(1, d), mod, w, col_scale.reshape(1, -1))


def _q_col_scale(n_cols, q_ranges):
    scale = jnp.ones((n_cols,), F32)
    for c0, c1 in q_ranges:
        scale = scale.at[c0:c1].set(Q_SCALE)
    return scale


def _post_kernel(*refs, dilations, merged, final, d_ff, chunks):
    n = len(dilations)
    o_refs = refs[:n]
    m_refs = refs[n:2 * n] if merged else ()
    l_refs = refs[2 * n:3 * n] if merged else refs[n:2 * n]
    k = 3 * n if merged else 2 * n
    e_ref, x_ref, mod_ref, wo_ref, g_ref, win_ref, wout_ref, gf_ref, out_ref = refs[k:k + 9]
    scratch = list(refs[k + 9:])
    tm = x_ref.shape[0]

    def natural_order(ref, dil):
        if dil == 1:
            return ref[0, 0].astype(F32)
        scr_ref = scratch.pop(0)
        n_lane_tiles = scr_ref.shape[0]
        for r in range(dil):
            part = ref[0, r].astype(F32)
            for t in range(n_lane_tiles):
                scr_ref[t, pl.ds(r, tm // dil, stride=dil), :] = part[:, t * LANES:(t + 1) * LANES]
        return jnp.concatenate([scr_ref[t] for t in range(n_lane_tiles)], axis=1)

    def expand(wgt):
        hi = wgt.astype(BF16)
        lo = (wgt - hi.astype(F32)).astype(BF16)
        return jnp.dot(jnp.concatenate([hi, lo], axis=1), e_ref[...], preferred_element_type=F32)

    is_head = lax.broadcasted_iota(jnp.int32, (tm, STAT_LANES), 1) < N_HEADS
    outs = [natural_order(r, dil) for r, dil in zip(o_refs, dilations)]
    dens = [jnp.where(is_head, natural_order(r, dil), 1.0) for r, dil in zip(l_refs, dilations)]
    if merged:
        maxes = [natural_order(r, dil) for r, dil in zip(m_refs, dilations)]
        top = functools.reduce(jnp.maximum, maxes)
        es = [jnp.exp2(mg - top) for mg in maxes]
        inv = 1.0 / functools.reduce(jnp.add, [e * l for e, l in zip(es, dens)])
        att = functools.reduce(jnp.add, [expand(jnp.where(is_head, e * inv, 0.0)) * o for e, o in zip(es, outs)])
    else:
        att = expand(jnp.where(is_head, 1.0 / dens[0], 0.0)) * outs[0]

    mod = mod_ref[0]
    x = x_ref[...] + mod[2:3] * jnp.dot(att.astype(BF16), wo_ref[...], preferred_element_type=F32)

    h = _modulated_norm(x, g_ref[...], mod, 3, 4).astype(BF16)
    acc = None
    for c0, c1 in chunks:
        gate = jnp.dot(h, win_ref[:, c0:c1], preferred_element_type=F32)
        up = jnp.dot(h, win_ref[:, d_ff + c0:d_ff + c1], preferred_element_type=F32)
        act = (gate * jax.nn.sigmoid(gate) * up).astype(BF16)
        part = jnp.dot(act, wout_ref[c0:c1, :], preferred_element_type=F32)
        acc = part if acc is None else acc + part
    y = x + mod[5:6] * acc
    if final:
        ms = jnp.mean(y * y, axis=-1, keepdims=True)
        y = y * lax.rsqrt(ms + EPS) * gf_ref[...]
    out_ref[...] = y


def _post(outs, maxes, dens, dilations, x, mod, w_o, g, w_in, w_out, g_final, seq, *, final, tm=512, chunk=512):
    m, d = x.shape
    d_ff = w_out.shape[0]
    per_seq = seq // tm
    merged = maxes is not None
    head_of_lane = jnp.arange(d) // HEAD_DIM
    expand = (jnp.arange(STAT_LANES)[:, None] == head_of_lane[None, :]).astype(BF16)
    expand2 = jnp.concatenate([expand, expand], axis=0)
    chunks = tuple((c0, min(c0 + chunk, d_ff)) for c0 in range(0, d_ff, chunk))
    row = lambda i: (i, 0)
    grouped = lambda dil, cols: pl.BlockSpec((1, dil, tm // dil, cols), lambda i: (i // per_seq, 0, i % per_seq, 0))
    n_stats = 2 if merged else 1
    scratch = []
    for dil in dilations:
        if dil > 1:
            scratch.append(pltpu.VMEM((d // LANES, tm, LANES), F32))
    for _ in range(n_stats):
        for dil in dilations:
            if dil > 1:
                scratch.append(pltpu.VMEM((1, tm, STAT_LANES), F32))
    stats = (list(maxes) if merged else []) + list(dens)
    return pl.pallas_call(
        functools.partial(_post_kernel, dilations=dilations, merged=merged, final=final, d_ff=d_ff, chunks=chunks),
        grid=(m // tm,),
        in_specs=[grouped(dil, d) for dil in dilations]
                 + [grouped(dil, STAT_LANES) for _ in range(n_stats) for dil in dilations]
                 + [_resident((2 * STAT_LANES, d)),
                    pl.BlockSpec((tm, d), row),
                    pl.BlockSpec((1, 6, d), lambda i: (i // per_seq, 0, 0)),
                    _resident((d, d)),
                    _resident((1, d)),
                    _resident((d, 2 * d_ff)),
                    _resident((d_ff, d)),
                    _resident((1, d))],
        out_specs=pl.BlockSpec((tm, d), row),
        out_shape=jax.ShapeDtypeStruct((m, d), F32),
        scratch_shapes=scratch,
        compiler_params=_cparams(1),
        name="post_attn_ffn",
    )(*outs, *stats, expand2, x, mod, w_o, g.reshape(1, d), w_in, w_out, g_final.reshape(1, d))


def _t5_bucket(rel):
    half = T5_BUCKETS // 2
    max_exact = half // 2
    ret = jnp.where(rel > 0, half, 0)
    n = jnp.abs(rel)
    nf = jnp.maximum(n, 1).astype(F32)
    large = max_exact + (jnp.log(nf / max_exact) / math.log(T5_MAX_DISTANCE / max_exact)
                         * (half - max_exact)).astype(jnp.int32)
    large = jnp.minimum(large, half - 1)
    return ret + jnp.where(n < max_exact, n, large)


def _t5_table_kernel(tab_ref, idx_ref, o_ref, *, permute_heads):
    idx = idx_ref[0]
    hits = [idx == b for b in range(T5_BUCKETS)]
    for p in range(N_HEADS):
        head = (p % HEADS_PER_BLOCK) * N_KV_HEADS + p // HEADS_PER_BLOCK if permute_heads else p
        acc = jnp.full(idx.shape, NEG_INF, F32)
        for b in range(T5_BUCKETS):
            acc = jnp.where(hits[b], tab_ref[b * N_HEADS + head] * LOG2_E, acc)
        o_ref[0, p] = acc


def _t5_table(rel_bias, radius, dilation, width, permute_heads=False):
    offs = jnp.arange(3)[:, None, None] * radius
    rel = jnp.arange(width)[None, None, :] - offs - jnp.arange(Q_BLOCK)[None, :, None]
    idx = jnp.where(jnp.abs(rel) <= radius, _t5_bucket(rel * dilation), -1).astype(jnp.int32)
    return pl.pallas_call(
        functools.partial(_t5_table_kernel, permute_heads=permute_heads),
        grid=(3,),
        in_specs=[pl.BlockSpec(memory_space=pltpu.SMEM),
                  pl.BlockSpec((1, Q_BLOCK, width), lambda v: (v, 0, 0))],
        out_specs=pl.BlockSpec((1, N_HEADS, Q_BLOCK, width), lambda v: (v, 0, 0, 0)),
        out_shape=jax.ShapeDtypeStruct((3, N_HEADS, Q_BLOCK, width), F32),
        compiler_params=_cparams(1),
        name="t5_table",
    )(rel_bias.reshape(-1), idx)


def _na_table_kernel(rpb_ref, o_ref):
    head = pl.program_id(0)
    shape = (GRID_W, 2 * GRID_W)
    w = lax.broadcasted_iota(jnp.int32, shape, 0)
    lane = lax.broadcasted_iota(jnp.int32, shape, 1)
    c = lax.bitwise_and(lane, GRID_W - 1)
    half = lax.shift_right_logical(lane, int(math.log2(GRID_W)))
    col_start = jnp.clip(w - NA_COLS // 2, 0, GRID_W - NA_COLS)
    valid = (c >= col_start) & (c < col_start + NA_COLS)
    col_off = jnp.where(valid, c - w + (NA_COLS - 1), -1)
    hits = [[(col_off == cv) & (half == hv) for cv in range(NA_COL_OFFS)] for hv in range(2)]
    for t in range(NA_ROW_OFFS - 1):
        acc = jnp.full(shape, NEG_INF, F32)
        for hv in range(2):
            for cv in range(NA_COL_OFFS):
                val = rpb_ref[(head * NA_ROW_OFFS + t + hv) * NA_COL_OFFS + cv]
                acc = jnp.where(hits[hv][cv], val * LOG2_E, acc)
        o_ref[0, t] = acc


def _na_table(rpb):
    n_t = NA_ROW_OFFS - 1
    return pl.pallas_call(
        _na_table_kernel,
        grid=(N_HEADS,),
        in_specs=[pl.BlockSpec(memory_space=pltpu.SMEM)],
        out_specs=pl.BlockSpec((1, n_t, GRID_W, 2 * GRID_W), lambda h: (h, 0, 0, 0)),
        out_shape=jax.ShapeDtypeStruct((N_HEADS, n_t, GRID_W, 2 * GRID_W), F32),
        compiler_params=_cparams(1),
        name="na_table",
    )(rpb.reshape(-1))


def _attend(q, k, v, rows, bias_fn, sink_fn):
    head_of_lane = lax.shift_right_logical(lax.broadcasted_iota(jnp.int32, (rows, LANE_BLOCK), 1),
                                           int(math.log2(HEAD_DIM)))
    stacked = jnp.concatenate(
        [q * jnp.where(head_of_lane == j, 1.0, 0.0).astype(BF16) for j in range(HEADS_PER_BLOCK)], axis=0)
    s = lax.dot_general(stacked, k, (((1,), (1,)), ((), ())), preferred_element_type=F32)
    ps, maxes, dens = [], [], []
    for j in range(HEADS_PER_BLOCK):
        sj = s[j * rows:(j + 1) * rows] + bias_fn(j)
        m = jnp.max(sj, axis=-1, keepdims=True)
        if sink_fn is not None:
            m = jnp.maximum(m, sink_fn(j))
        p = jnp.exp2(sj - m)
        l = jnp.sum(p, axis=-1, keepdims=True)
        if sink_fn is not None:
            l = l + jnp.exp2(sink_fn(j) - m)
        ps.append(p.astype(BF16))
        maxes.append(m)
        dens.append(l)
    pv = jnp.dot(jnp.concatenate(ps, axis=0), v, preferred_element_type=F32)
    out = pv[0:rows]
    for j in range(1, HEADS_PER_BLOCK):
        out = jnp.where(head_of_lane == j, pv[j * rows:(j + 1) * rows], out)
    return out, maxes, dens


def _stat_tile(cols, tile):
    lane = lax.broadcasted_iota(jnp.int32, tile, 1)
    out = jnp.zeros(tile, F32)
    for col, val in cols:
        out = jnp.where(lane == col, val, out)
    return out


def _band_attn_kernel(*refs, n_blk, radius, width, has_sink, want_max, unroll):
    if has_sink:
        sink_ref, refs = refs[0], refs[1:]
    q_ref, k_ref, v_ref, tab_ref, o_ref = refs[:5]
    stat_refs = refs[5:]
    n_seqs, seq = q_ref.shape[0], q_ref.shape[1]
    n_steps = seq // Q_BLOCK
    blk0 = pl.program_id(2) * n_blk
    accumulate = n_blk < N_HEAD_BLOCKS

    if accumulate:
        @pl.when(pl.program_id(2) == 0)
        def _():
            for ref in stat_refs:
                ref[...] = jnp.zeros(ref.shape, F32)

    def step(t, carry):
        g, n = t // n_steps, t % n_steps
        q0 = pl.multiple_of(n * Q_BLOCK, Q_BLOCK)
        k0 = pl.multiple_of(jnp.clip(n * Q_BLOCK - radius, 0, seq - width), HEAD_DIM)
        variant = jnp.where(n == 0, 0, jnp.where(n == n_steps - 1, 2, 1))
        max_cols, den_cols = [], []
        for b in range(n_blk):
            lanes = slice(b * LANE_BLOCK, (b + 1) * LANE_BLOCK)
            klanes = lanes if k_ref.shape[2] > LANE_BLOCK else slice(0, LANE_BLOCK)
            q = q_ref[g, pl.ds(q0, Q_BLOCK), lanes]
            k = k_ref[g, pl.ds(k0, width), klanes]
            v = v_ref[g, pl.ds(k0, width), klanes]
            bias_fn = lambda j: tab_ref[variant, b * HEADS_PER_BLOCK + j]
            sink_fn = None
            if has_sink:
                sink_fn = lambda j: sink_ref[j * N_KV_HEADS + blk0 + b] * LOG2_E
            out, maxes, dens = _attend(q, k, v, Q_BLOCK, bias_fn, sink_fn)
            o_ref[g, pl.ds(q0, Q_BLOCK), lanes] = out.astype(o_ref.dtype)
            for j in range(HEADS_PER_BLOCK):
                col = (blk0 + b) * HEADS_PER_BLOCK + j
                max_cols.append((col, maxes[j]))
                den_cols.append((col, dens[j]))
        for ref, cols in zip(stat_refs, ([max_cols] if want_max else []) + [den_cols]):
            tile = _stat_tile(cols, (Q_BLOCK, STAT_LANES))
            if accumulate:
                tile = tile + ref[g, pl.ds(q0, Q_BLOCK), :]
            ref[g, pl.ds(q0, Q_BLOCK), :] = tile
        return carry

    lax.fori_loop(0, n_seqs * n_steps, step, 0, unroll=unroll)


def _dilated_attention(qkv, table, dilation):
    batch, _, sub, n_cols = qkv.shape
    d_model = n_cols // 3
    n_seq = batch * dilation
    n_blk = 1 if dilation == 1 else N_HEAD_BLOCKS
    wide = n_blk * LANE_BLOCK
    stride = d_model // wide
    view = qkv.reshape(n_seq, sub, n_cols)
    n_steps = sub // Q_BLOCK
    per_step = max(1, ATTN_UNITS_IN_FLIGHT // (n_steps * n_blk))
    unroll = min(per_step * n_steps, ATTN_UNITS_IN_FLIGHT // n_blk)

    def col(part):
        return lambda s, _, h: (s, 0, part * stride + h)

    stat_spec = pl.BlockSpec((per_step, sub, STAT_LANES), lambda s, _, h: (s, 0, 0))
    stat_shape = jax.ShapeDtypeStruct((n_seq, sub, STAT_LANES), F32)
    o, mx, den = pl.pallas_call(
        functools.partial(_band_attn_kernel, n_blk=n_blk, radius=HEAD_DIM, width=2 * Q_BLOCK,
                          has_sink=False, want_max=True, unroll=unroll),
        grid=(n_seq // per_step, 1, N_HEAD_BLOCKS // n_blk),
        in_specs=[pl.BlockSpec((per_step, sub, wide), col(0)),
                  pl.BlockSpec((per_step, sub, wide), col(1)),
                  pl.BlockSpec((per_step, sub, wide), col(2)),
                  pl.BlockSpec((3, n_blk * HEADS_PER_BLOCK, Q_BLOCK, 2 * Q_BLOCK), lambda s, _, h: (0, h, 0, 0))],
        out_specs=[pl.BlockSpec((per_step, sub, wide), lambda s, _, h: (s, 0, h)), stat_spec, stat_spec],
        out_shape=[jax.ShapeDtypeStruct((n_seq, sub, d_model), BF16), stat_shape, stat_shape],
        compiler_params=_cparams(3),
        name=f"dilated_attn_d{dilation}",
    )(view, view, view, table)
    grouped = lambda a: a.reshape(batch, dilation, sub, a.shape[-1])
    return grouped(o), grouped(mx), grouped(den)


def _window_gqa_attention(qkv, table, sink, batch, seq):
    d_model = N_HEADS * HEAD_DIM
    view = qkv.reshape(batch, seq, qkv.shape[-1])
    k_blk = d_model // LANE_BLOCK
    o, den = pl.pallas_call(
        functools.partial(_band_attn_kernel, n_blk=1, radius=SWA_RADIUS, width=3 * Q_BLOCK,
                          has_sink=True, want_max=False, unroll=ATTN_UNITS_IN_FLIGHT // 2),
        grid=(batch, 1, N_HEAD_BLOCKS),
        in_specs=[pl.BlockSpec(memory_space=pltpu.SMEM),
                  pl.BlockSpec((1, seq, LANE_BLOCK), lambda b, r, h: (b, 0, h)),
                  pl.BlockSpec((1, seq, LANE_BLOCK), lambda b, r, h: (b, 0, k_blk)),
                  pl.BlockSpec((1, seq, LANE_BLOCK), lambda b, r, h: (b, 0, k_blk + 1)),
                  pl.BlockSpec((3, HEADS_PER_BLOCK, Q_BLOCK, 3 * Q_BLOCK), lambda b, r, h: (0, h, 0, 0))],
        out_specs=[pl.BlockSpec((1, seq, LANE_BLOCK), lambda b, r, h: (b, 0, h)),
                   pl.BlockSpec((1, seq, STAT_LANES), lambda b, r, h: (b, 0, 0))],
        out_shape=[jax.ShapeDtypeStruct((batch, seq, d_model), BF16),
                   jax.ShapeDtypeStruct((batch, seq, STAT_LANES), F32)],
        compiler_params=_cparams(3),
        name="window_gqa_attn",
    )(sink, view, view, view, table)
    return o.reshape(batch, 1, seq, d_model), den.reshape(batch, 1, seq, STAT_LANES)


def _nbr_attn_kernel(q_ref, k_ref, v_ref, tab_ref, o_ref, den_ref):
    n_rows = q_ref.shape[1] // GRID_W
    n_keys = NA_ROWS * GRID_W
    blk = pl.program_id(1)

    @pl.when(blk == 0)
    def _():
        den_ref[...] = jnp.zeros(den_ref.shape, F32)

    def step(r, carry):
        row_start = jnp.clip(r - NA_ROWS // 2, 0, n_rows - NA_ROWS)
        t0 = row_start - r + (NA_ROWS - 1)
        q0 = pl.multiple_of(r * GRID_W, GRID_W)
        k0 = pl.multiple_of(row_start * GRID_W, GRID_W)
        q = q_ref[0, pl.ds(q0, GRID_W), :]
        k = k_ref[0, pl.ds(k0, n_keys), :]
        v = v_ref[0, pl.ds(k0, n_keys), :]
        bias_fn = lambda j: jnp.concatenate([tab_ref[j, t0 + 2 * pr] for pr in range(NA_ROWS // 2)], axis=1)
        out, _, dens = _attend(q, k, v, GRID_W, bias_fn, None)
        o_ref[0, pl.ds(q0, GRID_W), :] = out.astype(o_ref.dtype)
        cols = [(blk * HEADS_PER_BLOCK + j, dens[j]) for j in range(HEADS_PER_BLOCK)]
        den_ref[0, pl.ds(q0, GRID_W), :] = den_ref[0, pl.ds(q0, GRID_W), :] + _stat_tile(cols, (GRID_W, STAT_LANES))
        return carry

    lax.fori_loop(0, n_rows, step, 0, unroll=ATTN_UNITS_IN_FLIGHT)


def _neighbourhood_attention(qkv, table, batch, seq):
    d_model = N_HEADS * HEAD_DIM
    view = qkv.reshape(batch, seq, 3 * d_model)
    nb = N_HEAD_BLOCKS
    blk = lambda part: pl.BlockSpec((1, seq, LANE_BLOCK), lambda b, h: (b, 0, part * nb + h))
    o, den = pl.pallas_call(
        _nbr_attn_kernel,
        grid=(batch, nb),
        in_specs=[blk(0), blk(1), blk(2),
                  pl.BlockSpec((HEADS_PER_BLOCK, NA_ROW_OFFS - 1, GRID_W, 2 * GRID_W), lambda b, h: (h, 0, 0, 0))],
        out_specs=[pl.BlockSpec((1, seq, LANE_BLOCK), lambda b, h: (b, 0, h)),
                   pl.BlockSpec((1, seq, STAT_LANES), lambda b, h: (b, 0, 0))],
        out_shape=[jax.ShapeDtypeStruct((batch, seq, d_model), BF16),
                   jax.ShapeDtypeStruct((batch, seq, STAT_LANES), F32)],
        compiler_params=_cparams(2),
        name="neighbourhood_attn",
    )(view, view, view, table)
    return o.reshape(batch, 1, seq, d_model), den.reshape(batch, 1, seq, STAT_LANES)


def kernel(x, c, rel_bias, ada_w, ada_b, norm_mix, norm_ffn, norm_final, a_w_in, a_w_out, b_w_in, b_w_out, b_rpb, c_w_in, c_w_out, c_sink, ffn_w_in, ffn_w_out):
    batch, seq, d_model = x.shape
    depth = ada_w.shape[0]
    assert d_model == N_HEADS * HEAD_DIM and seq % (16 * 2 * Q_BLOCK) == 0 and seq // GRID_W >= NA_ROWS

    mods = _ada(c, ada_w, ada_b).reshape(depth, batch, 6, d_model)
    xf = x.reshape(batch * seq, d_model)

    dils = tuple(dil for _, dil in DILATED_GROUPS)
    tabs_a = [_t5_table(rel_bias, HEAD_DIM, dil, 2 * Q_BLOCK) for dil in dils]
    tab_c = _t5_table(rel_bias, SWA_RADIUS, 1, 3 * Q_BLOCK, permute_heads=True)
    perm = jnp.array([(p % HEADS_PER_BLOCK) * N_KV_HEADS + p // HEADS_PER_BLOCK for p in range(N_HEADS)])

    for i in range(depth):
        kind, j = i % 3, i // 3
        mod = mods[i]
        if kind == 0:
            n_in = a_w_in.shape[2]
            scale = _q_col_scale(n_in, [(g * 3 * d_model, g * 3 * d_model + d_model) for g in range(len(dils))])
            qkvs = _proj(xf, norm_mix[i], mod, a_w_in[j].astype(BF16), scale, batch, seq, dils)
            outs, maxes, dens = zip(*[_dilated_attention(qkv, tab, dil) for qkv, tab, dil in zip(qkvs, tabs_a, dils)])
            post_dils, w_o = dils, a_w_out[j].astype(BF16)
        elif kind == 1:
            scale = _q_col_scale(3 * d_model, [(0, d_model)])
            [qkv] = _proj(xf, norm_mix[i], mod, b_w_in[j].astype(BF16), scale, batch, seq)
            o, den = _neighbourhood_attention(qkv, _na_table(b_rpb[j]), batch, seq)
            outs, maxes, dens, post_dils, w_o = [o], None, [den], (1,), b_w_out[j].astype(BF16)
        else:
            w_q = c_w_in[j][:, :d_model].reshape(d_model, N_HEADS, HEAD_DIM)[:, perm].reshape(d_model, d_model)
            w_in = jnp.concatenate([w_q, c_w_in[j][:, d_model:]], axis=1).astype(BF16)
            w_o = c_w_out[j].reshape(N_HEADS, HEAD_DIM, d_model)[perm].reshape(d_model, d_model).astype(BF16)
            scale = _q_col_scale(w_in.shape[1], [(0, d_model)])
            [qkv] = _proj(xf, norm_mix[i], mod, w_in, scale, batch, seq)
            o, den = _window_gqa_attention(qkv, tab_c, c_sink[j], batch, seq)
            outs, maxes, dens, post_dils = [o], None, [den], (1,)
        xf = _post(outs, maxes, dens, post_dils, xf, mod, w_o, norm_ffn[i], ffn_w_in[i].astype(BF16),
                   ffn_w_out[i].astype(BF16), norm_final, seq, final=(i == depth - 1))
    return xf.reshape(batch, seq, d_model)
```

```python
import functools
import math

import jax
import jax.numpy as jnp
from jax import lax
from jax.experimental import pallas as pl
from jax.experimental.pallas import tpu as pltpu

F32 = jnp.float32
BF16 = jnp.bfloat16

HEAD_DIM = 64
N_HEADS = 16
LANES = 128
HEADS_PER_BLOCK = 4
LANE_BLOCK = HEADS_PER_BLOCK * HEAD_DIM
N_HEAD_BLOCKS = N_HEADS // HEADS_PER_BLOCK
DILATED_GROUPS = ((128, 1), (512, 4), (2048, 16))
GRID_W = 64
NA_ROWS = 8
NA_COLS = 16
NA_ROW_OFFS = 2 * NA_ROWS - 1
NA_COL_OFFS = 2 * NA_COLS - 1
SWA_RADIUS = 128
N_KV_HEADS = 4
T5_BUCKETS = 32
T5_MAX_DISTANCE = 1024
EPS = 1e-6
NEG_INF = -1e30
LOG2_E = math.log2(math.e)
Q_SCALE = LOG2_E / math.sqrt(HEAD_DIM)
STAT_LANES = LANES
Q_BLOCK = 128
ATTN_UNITS_IN_FLIGHT = 64
VMEM_LIMIT = 58 * 1024 * 1024


def _cparams(n_axes, vmem=VMEM_LIMIT):
    return pltpu.CompilerParams(dimension_semantics=("arbitrary",) * n_axes, vmem_limit_bytes=vmem)


def _resident(shape):
    return pl.BlockSpec(shape, lambda *_: (0,) * len(shape), pipeline_mode=pl.Buffered(1))


def _ada_kernel(c_ref, w_ref, b_ref, o_ref):
    c = c_ref[...]
    cond = c * jax.nn.sigmoid(c)
    o_ref[0] = jnp.dot(cond.astype(BF16), w_ref[0].astype(BF16), preferred_element_type=F32) + b_ref[0]


def _ada(c, ada_w, ada_b, tn=1536):
    depth, d, n = ada_w.shape
    b = c.shape[0]
    return pl.pallas_call(
        _ada_kernel,
        grid=(depth, n // tn),
        in_specs=[pl.BlockSpec((b, d), lambda l, j: (0, 0)),
                  pl.BlockSpec((1, d, tn), lambda l, j: (l, 0, j)),
                  pl.BlockSpec((1, 1, tn), lambda l, j: (l, 0, j))],
        out_specs=pl.BlockSpec((1, b, tn), lambda l, j: (l, 0, j)),
        out_shape=jax.ShapeDtypeStruct((depth, b, n), F32),
        compiler_params=_cparams(2),
        name="ada_mod",
    )(c, ada_w, ada_b.reshape(depth, 1, n))


def _modulated_norm(x, g, mod, shift_row, scale_row):
    ms = jnp.mean(x * x, axis=-1, keepdims=True)
    y = x * lax.rsqrt(ms + EPS) * g
    return y * (1.0 + mod[scale_row:scale_row + 1]) + mod[shift_row:shift_row + 1]


def _proj_kernel(x_ref, g_ref, mod_ref, w_ref, cs_ref, *rest, dilations, n_chunk):
    out_refs = rest[:len(dilations)]
    y = _modulated_norm(x_ref[...], g_ref[...], mod_ref[0], 0, 1)
    tm = y.shape[0]
    n_lane_tiles = y.shape[1] // LANES
    if max(dilations) > 1:
        hn_ref = rest[len(dilations)]
        for t in range(n_lane_tiles):
            hn_ref[t] = y[:, t * LANES:(t + 1) * LANES]
    n_group = w_ref.shape[1] // len(dilations)
    for grp, dil in enumerate(dilations):
        rows = tm // dil
        if dil == 1:
            h = y.astype(BF16)
        else:
            h = jnp.concatenate(
                [jnp.concatenate([hn_ref[t, pl.ds(r, rows, stride=dil), :] for t in range(n_lane_tiles)], axis=1)
                 for r in range(dil)], axis=0).astype(BF16)
        for c0 in range(0, n_group, n_chunk):
            w0 = grp * n_group + c0
            res = jnp.dot(h, w_ref[:, w0:w0 + n_chunk], preferred_element_type=F32)
            res = (res * cs_ref[:, w0:w0 + n_chunk]).astype(BF16)
            for r in range(dil):
                out_refs[grp][0, r, :, c0:c0 + n_chunk] = res[r * rows:(r + 1) * rows]


def _proj(x, g, mod, w, col_scale, batch, seq, dilations=(1,), *, tm=512, n_chunk=512):
    m, d = x.shape
    n_group = w.shape[1] // len(dilations)
    per_seq = seq // tm
    scratch = [pltpu.VMEM((d // LANES, tm, LANES), F32)] if max(dilations) > 1 else []
    return pl.pallas_call(
        functools.partial(_proj_kernel, dilations=dilations, n_chunk=n_chunk),
        grid=(m // tm,),
        in_specs=[pl.BlockSpec((tm, d), lambda i: (i, 0)),
                  _resident((1, d)),
                  pl.BlockSpec((1, 6, d), lambda i: (i // per_seq, 0, 0)),
                  _resident(w.shape),
                  _resident((1, w.shape[1]))],
        out_specs=[pl.BlockSpec((1, dil, tm // dil, n_group), lambda i: (i // per_seq, 0, i % per_seq, 0))
                   for dil in dilations],
        out_shape=[jax.ShapeDtypeStruct((batch, dil, seq // dil, n_group), BF16) for dil in dilations],
        scratch_shapes=scratch,
        compiler_params=_cparams(1),
        name="norm_proj",
    )(x, g.reshape(1, d), mod, w, col_scale.reshape(1, -1))


def _q_col_scale(n_cols, q_ranges):
    scale = jnp.ones((n_cols,), F32)
    for c0, c1 in q_ranges:
        scale = scale.at[c0:c1].set(Q_SCALE)
    return scale


def _post_kernel(*refs, dilations, merged, final, d_ff, chunks):
    n = len(dilations)
    o_refs = refs[:n]
    m_refs = refs[n:2 * n] if merged else ()
    l_refs = refs[2 * n:3 * n] if merged else refs[n:2 * n]
    k = 3 * n if merged else 2 * n
    e_ref, x_ref, mod_ref, wo_ref, g_ref, win_ref, wout_ref, gf_ref, out_ref = refs[k:k + 9]
    scratch = list(refs[k + 9:])
    tm = x_ref.shape[0]

    def natural_order(ref, dil):
        if dil == 1:
            return ref[0, 0].astype(F32)
        scr_ref = scratch.pop(0)
        n_lane_tiles = scr_ref.shape[0]
        for r in range(dil):
            part = ref[0, r].astype(F32)
            for t in range(n_lane_tiles):
                scr_ref[t, pl.ds(r, tm // dil, stride=dil), :] = part[:, t * LANES:(t + 1) * LANES]
        return jnp.concatenate([scr_ref[t] for t in range(n_lane_tiles)], axis=1)

    def expand(wgt):
        hi = wgt.astype(BF16)
        lo = (wgt - hi.astype(F32)).astype(BF16)
        return jnp.dot(jnp.concatenate([hi, lo], axis=1), e_ref[...], preferred_element_type=F32)

    is_head = lax.broadcasted_iota(jnp.int32, (tm, STAT_LANES), 1) < N_HEADS
    outs = [natural_order(r, dil) for r, dil in zip(o_refs, dilations)]
    dens = [jnp.where(is_head, natural_order(r, dil), 1.0) for r, dil in zip(l_refs, dilations)]
    if merged:
        maxes = [natural_order(r, dil) for r, dil in zip(m_refs, dilations)]
        top = functools.reduce(jnp.maximum, maxes)
        es = [jnp.exp2(mg - top) for mg in maxes]
        inv = 1.0 / functools.reduce(jnp.add, [e * l for e, l in zip(es, dens)])
        att = functools.reduce(jnp.add, [expand(jnp.where(is_head, e * inv, 0.0)) * o for e, o in zip(es, outs)])
    else:
        att = expand(jnp.where(is_head, 1.0 / dens[0], 0.0)) * outs[0]

    mod = mod_ref[0]
    x = x_ref[...] + mod[2:3] * jnp.dot(att.astype(BF16), wo_ref[...], preferred_element_type=F32)

    h = _modulated_norm(x, g_ref[...], mod, 3, 4).astype(BF16)
    acc = None
    for c0, c1 in chunks:
        gate = jnp.dot(h, win_ref[:, c0:c1], preferred_element_type=F32)
        up = jnp.dot(h, win_ref[:, d_ff + c0:d_ff + c1], preferred_element_type=F32)
        act = (gate * jax.nn.sigmoid(gate) * up).astype(BF16)
        part = jnp.dot(act, wout_ref[c0:c1, :], preferred_element_type=F32)
        acc = part if acc is None else acc + part
    y = x + mod[5:6] * acc
    if final:
        ms = jnp.mean(y * y, axis=-1, keepdims=True)
        y = y * lax.rsqrt(ms + EPS) * gf_ref[...]
    out_ref[...] = y


def _post(outs, maxes, dens, dilations, x, mod, w_o, g, w_in, w_out, g_final, seq, *, final, tm=512, chunk=512):
    m, d = x.shape
    d_ff = w_out.shape[0]
    per_seq = seq // tm
    merged = maxes is not None
    head_of_lane = jnp.arange(d) // HEAD_DIM
    expand = (jnp.arange(STAT_LANES)[:, None] == head_of_lane[None, :]).astype(BF16)
    expand2 = jnp.concatenate([expand, expand], axis=0)
    chunks = tuple((c0, min(c0 + chunk, d_ff)) for c0 in range(0, d_ff, chunk))
    row = lambda i: (i, 0)
    grouped = lambda dil, cols: pl.BlockSpec((1, dil, tm // dil, cols), lambda i: (i // per_seq, 0, i % per_seq, 0))
    n_stats = 2 if merged else 1
    scratch = []
    for dil in dilations:
        if dil > 1:
            scratch.append(pltpu.VMEM((d // LANES, tm, LANES), F32))
    for _ in range(n_stats):
        for dil in dilations:
            if dil > 1:
                scratch.append(pltpu.VMEM((1, tm, STAT_LANES), F32))
    stats = (list(maxes) if merged else []) + list(dens)
    return pl.pallas_call(
        functools.partial(_post_kernel, dilations=dilations, merged=merged, final=final, d_ff=d_ff, chunks=chunks),
        grid=(m // tm,),
        in_specs=[grouped(dil, d) for dil in dilations]
                 + [grouped(dil, STAT_LANES) for _ in range(n_stats) for dil in dilations]
                 + [_resident((2 * STAT_LANES, d)),
                    pl.BlockSpec((tm, d), row),
                    pl.BlockSpec((1, 6, d), lambda i: (i // per_seq, 0, 0)),
                    _resident((d, d)),
                    _resident((1, d)),
                    _resident((d, 2 * d_ff)),
                    _resident((d_ff, d)),
                    _resident((1, d))],
        out_specs=pl.BlockSpec((tm, d), row),
        out_shape=jax.ShapeDtypeStruct((m, d), F32),
        scratch_shapes=scratch,
        compiler_params=_cparams(1),
        name="post_attn_ffn",
    )(*outs, *stats, expand2, x, mod, w_o, g.reshape(1, d), w_in, w_out, g_final.reshape(1, d))


def _t5_bucket(rel):
    half = T5_BUCKETS // 2
    max_exact = half // 2
    ret = jnp.where(rel > 0, half, 0)
    n = jnp.abs(rel)
    nf = jnp.maximum(n, 1).astype(F32)
    large = max_exact + (jnp.log(nf / max_exact) / math.log(T5_MAX_DISTANCE / max_exact)
                         * (half - max_exact)).astype(jnp.int32)
    large = jnp.minimum(large, half - 1)
    return ret + jnp.where(n < max_exact, n, large)


def _t5_table_kernel(tab_ref, idx_ref, o_ref, *, permute_heads):
    idx = idx_ref[0]
    hits = [idx == b for b in range(T5_BUCKETS)]
    for p in range(N_HEADS):
        head = (p % HEADS_PER_BLOCK) * N_KV_HEADS + p // HEADS_PER_BLOCK if permute_heads else p
        acc = jnp.full(idx.shape, NEG_INF, F32)
        for b in range(T5_BUCKETS):
            acc = jnp.where(hits[b], tab_ref[b * N_HEADS + head] * LOG2_E, acc)
        o_ref[0, p] = acc


def _t5_table(rel_bias, radius, dilation, width, permute_heads=False):
    offs = jnp.arange(3)[:, None, None] * radius
    rel = jnp.arange(width)[None, None, :] - offs - jnp.arange(Q_BLOCK)[None, :, None]
    idx = jnp.where(jnp.abs(rel) <= radius, _t5_bucket(rel * dilation), -1).astype(jnp.int32)
    return pl.pallas_call(
        functools.partial(_t5_table_kernel, permute_heads=permute_heads),
        grid=(3,),
        in_specs=[pl.BlockSpec(memory_space=pltpu.SMEM),
                  pl.BlockSpec((1, Q_BLOCK, width), lambda v: (v, 0, 0))],
        out_specs=pl.BlockSpec((1, N_HEADS, Q_BLOCK, width), lambda v: (v, 0, 0, 0)),
        out_shape=jax.ShapeDtypeStruct((3, N_HEADS, Q_BLOCK, width), F32),
        compiler_params=_cparams(1),
        name="t5_table",
    )(rel_bias.reshape(-1), idx)


def _na_table_kernel(rpb_ref, o_ref):
    head = pl.program_id(0)
    shape = (GRID_W, 2 * GRID_W)
    w = lax.broadcasted_iota(jnp.int32, shape, 0)
    lane = lax.broadcasted_iota(jnp.int32, shape, 1)
    c = lax.bitwise_and(lane, GRID_W - 1)
    half = lax.shift_right_logical(lane, int(math.log2(GRID_W)))
    col_start = jnp.clip(w - NA_COLS // 2, 0, GRID_W - NA_COLS)
    valid = (c >= col_start) & (c < col_start + NA_COLS)
    col_off = jnp.where(valid, c - w + (NA_COLS - 1), -1)
    hits = [[(col_off == cv) & (half == hv) for cv in range(NA_COL_OFFS)] for hv in range(2)]
    for t in range(NA_ROW_OFFS - 1):
        acc = jnp.full(shape, NEG_INF, F32)
        for hv in range(2):
            for cv in range(NA_COL_OFFS):
                val = rpb_ref[(head * NA_ROW_OFFS + t + hv) * NA_COL_OFFS + cv]
                acc = jnp.where(hits[hv][cv], val * LOG2_E, acc)
        o_ref[0, t] = acc


def _na_table(rpb):
    n_t = NA_ROW_OFFS - 1
    return pl.pallas_call(
        _na_table_kernel,
        grid=(N_HEADS,),
        in_specs=[pl.BlockSpec(memory_space=pltpu.SMEM)],
        out_specs=pl.BlockSpec((1, n_t, GRID_W, 2 * GRID_W), lambda h: (h, 0, 0, 0)),
        out_shape=jax.ShapeDtypeStruct((N_HEADS, n_t, GRID_W, 2 * GRID_W), F32),
        compiler_params=_cparams(1),
        name="na_table",
    )(rpb.reshape(-1))


def _attend(q, k, v, rows, bias_fn, sink_fn):
    head_of_lane = lax.shift_right_logical(lax.broadcasted_iota(jnp.int32, (rows, LANE_BLOCK), 1),
                                           int(math.log2(HEAD_DIM)))
    stacked = jnp.concatenate(
        [q * jnp.where(head_of_lane == j, 1.0, 0.0).astype(BF16) for j in range(HEADS_PER_BLOCK)], axis=0)
    s = lax.dot_general(stacked, k, (((1,), (1,)), ((), ())), preferred_element_type=F32)
    ps, maxes, dens = [], [], []
    for j in range(HEADS_PER_BLOCK):
        sj = s[j * rows:(j + 1) * rows] + bias_fn(j)
        m = jnp.max(sj, axis=-1, keepdims=True)
        if sink_fn is not None:
            m = jnp.maximum(m, sink_fn(j))
        p = jnp.exp2(sj - m)
        l = jnp.sum(p, axis=-1, keepdims=True)
        if sink_fn is not None:
            l = l + jnp.exp2(sink_fn(j) - m)
        ps.append(p.astype(BF16))
        maxes.append(m)
        dens.append(l)
    pv = jnp.dot(jnp.concatenate(ps, axis=0), v, preferred_element_type=F32)
    out = pv[0:rows]
    for j in range(1, HEADS_PER_BLOCK):
        out = jnp.where(head_of_lane == j, pv[j * rows:(j + 1) * rows], out)
    return out, maxes, dens


def _stat_tile(cols, tile):
    lane = lax.broadcasted_iota(jnp.int32, tile, 1)
    out = jnp.zeros(tile, F32)
    for col, val in cols:
        out = jnp.where(lane == col, val, out)
    return out


def _band_attn_kernel(*refs, n_blk, radius, width, has_sink, want_max):
    if has_sink:
        sink_ref, refs = refs[0], refs[1:]
    q_ref, k_ref, v_ref, tab_ref, o_ref = refs[:5]
    stat_refs = refs[5:]
    n_seqs, seq = q_ref.shape[0], q_ref.shape[1]
    n_steps = seq // Q_BLOCK
    blk0 = pl.program_id(2) * n_blk
    accumulate = n_blk < N_HEAD_BLOCKS

    if accumulate:
        @pl.when(pl.program_id(2) == 0)
        def _():
            for ref in stat_refs:
                ref[...] = jnp.zeros(ref.shape, F32)

    for t in range(n_seqs * n_steps):
        g, n = t // n_steps, t % n_steps
        q0 = n * Q_BLOCK
        k0 = min(max(n * Q_BLOCK - radius, 0), seq - width)
        variant = 0 if n == 0 else (2 if n == n_steps - 1 else 1)
        max_cols, den_cols = [], []
        for b in range(n_blk):
            lanes = slice(b * LANE_BLOCK, (b + 1) * LANE_BLOCK)
            klanes = lanes if k_ref.shape[2] > LANE_BLOCK else slice(0, LANE_BLOCK)
            q = q_ref[g, pl.ds(q0, Q_BLOCK), lanes]
            k = k_ref[g, pl.ds(k0, width), klanes]
            v = v_ref[g, pl.ds(k0, width), klanes]
            bias_fn = lambda j: tab_ref[variant, b * HEADS_PER_BLOCK + j]
            sink_fn = None
            if has_sink:
                sink_fn = lambda j: sink_ref[j * N_KV_HEADS + blk0 + b] * LOG2_E
            out, maxes, dens = _attend(q, k, v, Q_BLOCK, bias_fn, sink_fn)
            o_ref[g, pl.ds(q0, Q_BLOCK), lanes] = out.astype(o_ref.dtype)
            for j in range(HEADS_PER_BLOCK):
                col = (blk0 + b) * HEADS_PER_BLOCK + j
                max_cols.append((col, maxes[j]))
                den_cols.append((col, dens[j]))
        for ref, cols in zip(stat_refs, ([max_cols] if want_max else []) + [den_cols]):
            tile = _stat_tile(cols, (Q_BLOCK, STAT_LANES))
            if accumulate:
                tile = tile + ref[g, pl.ds(q0, Q_BLOCK), :]
            ref[g, pl.ds(q0, Q_BLOCK), :] = tile


def _dilated_attention(qkv, table, dilation):
    batch, _, sub, n_cols = qkv.shape
    d_model = n_cols // 3
    n_seq = batch * dilation
    n_blk = 1 if dilation == 1 else N_HEAD_BLOCKS
    wide = n_blk * LANE_BLOCK
    stride = d_model // wide
    view = qkv.reshape(n_seq, sub, n_cols)
    n_steps = sub // Q_BLOCK
    per_step = max(1, ATTN_UNITS_IN_FLIGHT // (n_steps * n_blk))

    def col(part):
        return lambda s, _, h: (s, 0, part * stride + h)

    stat_spec = pl.BlockSpec((per_step, sub, STAT_LANES), lambda s, _, h: (s, 0, 0))
    stat_shape = jax.ShapeDtypeStruct((n_seq, sub, STAT_LANES), F32)
    o, mx, den = pl.pallas_call(
        functools.partial(_band_attn_kernel, n_blk=n_blk, radius=HEAD_DIM, width=2 * Q_BLOCK,
                          has_sink=False, want_max=True),
        grid=(n_seq // per_step, 1, N_HEAD_BLOCKS // n_blk),
        in_specs=[pl.BlockSpec((per_step, sub, wide), col(0)),
                  pl.BlockSpec((per_step, sub, wide), col(1)),
                  pl.BlockSpec((per_step, sub, wide), col(2)),
                  pl.BlockSpec((3, n_blk * HEADS_PER_BLOCK, Q_BLOCK, 2 * Q_BLOCK), lambda s, _, h: (0, h, 0, 0))],
        out_specs=[pl.BlockSpec((per_step, sub, wide), lambda s, _, h: (s, 0, h)), stat_spec, stat_spec],
        out_shape=[jax.ShapeDtypeStruct((n_seq, sub, d_model), BF16), stat_shape, stat_shape],
        compiler_params=_cparams(3),
        name=f"dilated_attn_d{dilation}",
    )(view, view, view, table)
    grouped = lambda a: a.reshape(batch, dilation, sub, a.shape[-1])
    return grouped(o), grouped(mx), grouped(den)


def _window_gqa_attention(qkv, table, sink, batch, seq):
    d_model = N_HEADS * HEAD_DIM
    view = qkv.reshape(batch, seq, qkv.shape[-1])
    k_blk = d_model // LANE_BLOCK
    o, den = pl.pallas_call(
        functools.partial(_band_attn_kernel, n_blk=1, radius=SWA_RADIUS, width=3 * Q_BLOCK,
                          has_sink=True, want_max=False),
        grid=(batch, 1, N_HEAD_BLOCKS),
        in_specs=[pl.BlockSpec(memory_space=pltpu.SMEM),
                  pl.BlockSpec((1, seq, LANE_BLOCK), lambda b, r, h: (b, 0, h)),
                  pl.BlockSpec((1, seq, LANE_BLOCK), lambda b, r, h: (b, 0, k_blk)),
                  pl.BlockSpec((1, seq, LANE_BLOCK), lambda b, r, h: (b, 0, k_blk + 1)),
                  pl.BlockSpec((3, HEADS_PER_BLOCK, Q_BLOCK, 3 * Q_BLOCK), lambda b, r, h: (0, h, 0, 0))],
        out_specs=[pl.BlockSpec((1, seq, LANE_BLOCK), lambda b, r, h: (b, 0, h)),
                   pl.BlockSpec((1, seq, STAT_LANES), lambda b, r, h: (b, 0, 0))],
        out_shape=[jax.ShapeDtypeStruct((batch, seq, d_model), BF16),
                   jax.ShapeDtypeStruct((batch, seq, STAT_LANES), F32)],
        compiler_params=_cparams(3),
        name="window_gqa_attn",
    )(sink, view, view, view, table)
    return o.reshape(batch, 1, seq, d_model), den.reshape(batch, 1, seq, STAT_LANES)


def _nbr_attn_kernel(q_ref, k_ref, v_ref, tab_ref, o_ref, den_ref):
    n_rows = q_ref.shape[1] // GRID_W
    n_keys = NA_ROWS * GRID_W
    blk = pl.program_id(1)

    @pl.when(blk == 0)
    def _():
        den_ref[...] = jnp.zeros(den_ref.shape, F32)

    for r in range(n_rows):
        row_start = min(max(r - NA_ROWS // 2, 0), n_rows - NA_ROWS)
        t0 = row_start - r + (NA_ROWS - 1)
        q0 = r * GRID_W
        k0 = row_start * GRID_W
        q = q_ref[0, pl.ds(q0, GRID_W), :]
        k = k_ref[0, pl.ds(k0, n_keys), :]
        v = v_ref[0, pl.ds(k0, n_keys), :]
        bias_fn = lambda j: jnp.concatenate([tab_ref[j, t0 + 2 * pr] for pr in range(NA_ROWS // 2)], axis=1)
        out, _, dens = _attend(q, k, v, GRID_W, bias_fn, None)
        o_ref[0, pl.ds(q0, GRID_W), :] = out.astype(o_ref.dtype)
        cols = [(blk * HEADS_PER_BLOCK + j, dens[j]) for j in range(HEADS_PER_BLOCK)]
        den_ref[0, pl.ds(q0, GRID_W), :] = den_ref[0, pl.ds(q0, GRID_W), :] + _stat_tile(cols, (GRID_W, STAT_LANES))


def _neighbourhood_attention(qkv, table, batch, seq):
    d_model = N_HEADS * HEAD_DIM
    view = qkv.reshape(batch, seq, 3 * d_model)
    nb = N_HEAD_BLOCKS
    blk = lambda part: pl.BlockSpec((1, seq, LANE_BLOCK), lambda b, h: (b, 0, part * nb + h))
    o, den = pl.pallas_call(
        _nbr_attn_kernel,
        grid=(batch, nb),
        in_specs=[blk(0), blk(1), blk(2),
                  pl.BlockSpec((HEADS_PER_BLOCK, NA_ROW_OFFS - 1, GRID_W, 2 * GRID_W), lambda b, h: (h, 0, 0, 0))],
        out_specs=[pl.BlockSpec((1, seq, LANE_BLOCK), lambda b, h: (b, 0, h)),
                   pl.BlockSpec((1, seq, STAT_LANES), lambda b, h: (b, 0, 0))],
        out_shape=[jax.ShapeDtypeStruct((batch, seq, d_model), BF16),
                   jax.ShapeDtypeStruct((batch, seq, STAT_LANES), F32)],
        compiler_params=_cparams(2),
        name="neighbourhood_attn",
    )(view, view, view, table)
    return o.reshape(batch, 1, seq, d_model), den.reshape(batch, 1, seq, STAT_LANES)


def kernel(x, c, rel_bias, ada_w, ada_b, norm_mix, norm_ffn, norm_final, a_w_in, a_w_out, b_w_in, b_w_out, b_rpb, c_w_in, c_w_out, c_sink, ffn_w_in, ffn_w_out):
    batch, seq, d_model = x.shape
    depth = ada_w.shape[0]
    assert d_model == N_HEADS * HEAD_DIM and seq % (16 * 2 * Q_BLOCK) == 0 and seq // GRID_W >= NA_ROWS

    mods = _ada(c, ada_w, ada_b).reshape(depth, batch, 6, d_model)
    xf = x.reshape(batch * seq, d_model)

    dils = tuple(dil for _, dil in DILATED_GROUPS)
    tabs_a = [_t5_table(rel_bias, HEAD_DIM, dil, 2 * Q_BLOCK) for dil in dils]
    tab_c = _t5_table(rel_bias, SWA_RADIUS, 1, 3 * Q_BLOCK, permute_heads=True)
    perm = jnp.array([(p % HEADS_PER_BLOCK) * N_KV_HEADS + p // HEADS_PER_BLOCK for p in range(N_HEADS)])

    for i in range(depth):
        kind, j = i % 3, i // 3
        mod = mods[i]
        if kind == 0:
            n_in = a_w_in.shape[2]
            scale = _q_col_scale(n_in, [(g * 3 * d_model, g * 3 * d_model + d_model) for g in range(len(dils))])
            qkvs = _proj(xf, norm_mix[i], mod, a_w_in[j].astype(BF16), scale, batch, seq, dils)
            outs, maxes, dens = zip(*[_dilated_attention(qkv, tab, dil) for qkv, tab, dil in zip(qkvs, tabs_a, dils)])
            post_dils, w_o = dils, a_w_out[j].astype(BF16)
        elif kind == 1:
            scale = _q_col_scale(3 * d_model, [(0, d_model)])
            [qkv] = _proj(xf, norm_mix[i], mod, b_w_in[j].astype(BF16), scale, batch, seq)
            o, den = _neighbourhood_attention(qkv, _na_table(b_rpb[j]), batch, seq)
            outs, maxes, dens, post_dils, w_o = [o], None, [den], (1,), b_w_out[j].astype(BF16)
        else:
            w_q = c_w_in[j][:, :d_model].reshape(d_model, N_HEADS, HEAD_DIM)[:, perm].reshape(d_model, d_model)
            w_in = jnp.concatenate([w_q, c_w_in[j][:, d_model:]], axis=1).astype(BF16)
            w_o = c_w_out[j].reshape(N_HEADS, HEAD_DIM, d_model)[perm].reshape(d_model, d_model).astype(BF16)
            scale = _q_col_scale(w_in.shape[1], [(0, d_model)])
            [qkv] = _proj(xf, norm_mix[i], mod, w_in, scale, batch, seq)
            o, den = _window_gqa_attention(qkv, tab_c, c_sink[j], batch, seq)
            outs, maxes, dens, post_dils = [o], None, [den], (1,)
        xf = _post(outs, maxes, dens, post_dils, xf, mod, w_o, norm_ffn[i], ffn_w_in[i].astype(BF16),
                   ffn_w_out[i].astype(BF16), norm_final, seq, final=(i == depth - 1))
    return xf.reshape(batch, seq, d_model)
```

```python
import functools
import math

import jax
import jax.numpy as jnp
from jax import lax
from jax.experimental import pallas as pl
from jax.experimental.pallas import tpu as pltpu

F32 = jnp.float32
BF16 = jnp.bfloat16

HEAD_DIM = 64
N_HEADS = 16
LANES = 128
HEADS_PER_BLOCK = 4
LANE_BLOCK = HEADS_PER_BLOCK * HEAD_DIM
N_HEAD_BLOCKS = N_HEADS // HEADS_PER_BLOCK
DILATED_GROUPS = ((128, 1), (512, 4), (2048, 16))
GRID_W = 64
NA_ROWS = 8
NA_COLS = 16
NA_ROW_OFFS = 2 * NA_ROWS - 1
NA_COL_OFFS = 2 * NA_COLS - 1
SWA_RADIUS = 128
N_KV_HEADS = 4
T5_BUCKETS = 32
T5_MAX_DISTANCE = 1024
EPS = 1e-6
NEG_INF = -1e30
LOG2_E = math.log2(math.e)
Q_SCALE = LOG2_E / math.sqrt(HEAD_DIM)
STAT_LANES = LANES
Q_BLOCK = 128
ATTN_UNITS_IN_FLIGHT = 64
VMEM_LIMIT = 58 * 1024 * 1024
PROJ_VMEM_BUDGET = 54 * 1024 * 1024


def _cparams(n_axes, vmem=VMEM_LIMIT):
    return pltpu.CompilerParams(dimension_semantics=("arbitrary",) * n_axes, vmem_limit_bytes=vmem)


def _resident(shape):
    return pl.BlockSpec(shape, lambda *_: (0,) * len(shape), pipeline_mode=pl.Buffered(1))


def _ada_kernel(c_ref, w_ref, b_ref, o_ref):
    c = c_ref[...]
    cond = c * jax.nn.sigmoid(c)
    o_ref[0] = jnp.dot(cond.astype(BF16), w_ref[0].astype(BF16), preferred_element_type=F32) + b_ref[0]


def _ada(c, ada_w, ada_b, tn=1536):
    depth, d, n = ada_w.shape
    b = c.shape[0]
    return pl.pallas_call(
        _ada_kernel,
        grid=(depth, n // tn),
        in_specs=[pl.BlockSpec((b, d), lambda l, j: (0, 0)),
                  pl.BlockSpec((1, d, tn), lambda l, j: (l, 0, j)),
                  pl.BlockSpec((1, 1, tn), lambda l, j: (l, 0, j))],
        out_specs=pl.BlockSpec((1, b, tn), lambda l, j: (l, 0, j)),
        out_shape=jax.ShapeDtypeStruct((depth, b, n), F32),
        compiler_params=_cparams(2),
        name="ada_mod",
    )(c, ada_w, ada_b.reshape(depth, 1, n))


def _modulated_norm(x, g, mod, shift_row, scale_row):
    ms = jnp.mean(x * x, axis=-1, keepdims=True)
    y = x * lax.rsqrt(ms + EPS) * g
    return y * (1.0 + mod[scale_row:scale_row + 1]) + mod[shift_row:shift_row + 1]


def _proj_kernel(x_ref, g_ref, mod_ref, w_ref, cs_ref, *rest, dilations, n_chunk):
    out_refs = rest[:len(dilations)]
    y = _modulated_norm(x_ref[...], g_ref[...], mod_ref[0], 0, 1)
    tm = y.shape[0]
    n_lane_tiles = y.shape[1] // LANES
    if max(dilations) > 1:
        hn_ref = rest[len(dilations)]
        for t in range(n_lane_tiles):
            hn_ref[t] = y[:, t * LANES:(t + 1) * LANES]
    n_group = w_ref.shape[1] // len(dilations)
    for grp, dil in enumerate(dilations):
        rows = tm // dil
        if dil == 1:
            h = y.astype(BF16)
        else:
            h = jnp.concatenate(
                [jnp.concatenate([hn_ref[t, pl.ds(r, rows, stride=dil), :] for t in range(n_lane_tiles)], axis=1)
                 for r in range(dil)], axis=0).astype(BF16)
        for c0 in range(0, n_group, n_chunk):
            w0 = grp * n_group + c0
            res = jnp.dot(h, w_ref[:, w0:w0 + n_chunk], preferred_element_type=F32)
            res = (res * cs_ref[:, w0:w0 + n_chunk]).astype(BF16)
            for r in range(dil):
                out_refs[grp][0, r, :, c0:c0 + n_chunk] = res[r * rows:(r + 1) * rows]


def _proj_rows(d, n, dilated, n_chunk):
    for tm in (1024, 512, 256):
        need = d * n * 2 + 2 * tm * (n * 2 + d * 4) + tm * d * 4 * (3 if dilated else 2) + 3 * tm * n_chunk * 4
        if need <= PROJ_VMEM_BUDGET:
            return tm
    return 128


def _proj(x, g, mod, w, col_scale, batch, seq, dilations=(1,), *, n_chunk=512):
    m, d = x.shape
    tm = _proj_rows(d, w.shape[1], max(dilations) > 1, n_chunk)
    n_group = w.shape[1] // len(dilations)
    per_seq = seq // tm
    scratch = [pltpu.VMEM((d // LANES, tm, LANES), F32)] if max(dilations) > 1 else []
    return pl.pallas_call(
        functools.partial(_proj_kernel, dilations=dilations, n_chunk=n_chunk),
        grid=(m // tm,),
        in_specs=[pl.BlockSpec((tm, d), lambda i: (i, 0)),
                  _resident((1, d)),
                  pl.BlockSpec((1, 6, d), lambda i: (i // per_seq, 0, 0)),
                  _resident(w.shape),
                  _resident((1, w.shape[1]))],
        out_specs=[pl.BlockSpec((1, dil, tm // dil, n_group), lambda i: (i // per_seq, 0, i % per_seq, 0))
                   for dil in dilations],
        out_shape=[jax.ShapeDtypeStruct((batch, dil, seq // dil, n_group), BF16) for dil in dilations],
        scratch_shapes=scratch,
        compiler_params=_cparams(1),
        name="norm_proj",
    )(x, g.reshape(1, d), mod, w, col_scale.reshape(1, -1))


def _q_col_scale(n_cols, q_ranges):
    scale = jnp.ones((n_cols,), F32)
    for c0, c1 in q_ranges:
        scale = scale.at[c0:c1].set(Q_SCALE)
    return scale


def _post_kernel(*refs, dilations, merged, final, d_ff, chunks):
    n = len(dilations)
    o_refs = refs[:n]
    m_refs = refs[n:2 * n] if merged else ()
    l_refs = refs[2 * n:3 * n] if merged else refs[n:2 * n]
    k = 3 * n if merged else 2 * n
    e_ref, x_ref, mod_ref, wo_ref, g_ref, win_ref, wout_ref, gf_ref, out_ref = refs[k:k + 9]
    scratch = list(refs[k + 9:])
    tm = x_ref.shape[0]

    def natural_order(ref, dil):
        if dil == 1:
            return ref[0, 0].astype(F32)
        scr_ref = scratch.pop(0)
        n_lane_tiles = scr_ref.shape[0]
        for r in range(dil):
            part = ref[0, r].astype(F32)
            for t in range(n_lane_tiles):
                scr_ref[t, pl.ds(r, tm // dil, stride=dil), :] = part[:, t * LANES:(t + 1) * LANES]
        return jnp.concatenate([scr_ref[t] for t in range(n_lane_tiles)], axis=1)

    def expand(wgt):
        hi = wgt.astype(BF16)
        lo = (wgt - hi.astype(F32)).astype(BF16)
        return jnp.dot(jnp.concatenate([hi, lo], axis=1), e_ref[...], preferred_element_type=F32)

    is_head = lax.broadcasted_iota(jnp.int32, (tm, STAT_LANES), 1) < N_HEADS
    outs = [natural_order(r, dil) for r, dil in zip(o_refs, dilations)]
    dens = [jnp.where(is_head, natural_order(r, dil), 1.0) for r, dil in zip(l_refs, dilations)]
    if merged:
        maxes = [natural_order(r, dil) for r, dil in zip(m_refs, dilations)]
        top = functools.reduce(jnp.maximum, maxes)
        es = [jnp.exp2(mg - top) for mg in maxes]
        inv = 1.0 / functools.reduce(jnp.add, [e * l for e, l in zip(es, dens)])
        att = functools.reduce(jnp.add, [expand(jnp.where(is_head, e * inv, 0.0)) * o for e, o in zip(es, outs)])
    else:
        att = expand(jnp.where(is_head, 1.0 / dens[0], 0.0)) * outs[0]

    mod = mod_ref[0]
    x = x_ref[...] + mod[2:3] * jnp.dot(att.astype(BF16), wo_ref[...], preferred_element_type=F32)

    h = _modulated_norm(x, g_ref[...], mod, 3, 4).astype(BF16)
    acc = None
    for c0, c1 in chunks:
        gate = jnp.dot(h, win_ref[:, c0:c1], preferred_element_type=F32)
        up = jnp.dot(h, win_ref[:, d_ff + c0:d_ff + c1], preferred_element_type=F32)
        act = (gate * jax.nn.sigmoid(gate) * up).astype(BF16)
        part = jnp.dot(act, wout_ref[c0:c1, :], preferred_element_type=F32)
        acc = part if acc is None else acc + part
    y = x + mod[5:6] * acc
    if final:
        ms = jnp.mean(y * y, axis=-1, keepdims=True)
        y = y * lax.rsqrt(ms + EPS) * gf_ref[...]
    out_ref[...] = y


def _post(outs, maxes, dens, dilations, x, mod, w_o, g, w_in, w_out, g_final, seq, *, final, tm=512, chunk=512):
    m, d = x.shape
    d_ff = w_out.shape[0]
    per_seq = seq // tm
    merged = maxes is not None
    head_of_lane = jnp.arange(d) // HEAD_DIM
    expand = (jnp.arange(STAT_LANES)[:, None] == head_of_lane[None, :]).astype(BF16)
    expand2 = jnp.concatenate([expand, expand], axis=0)
    chunks = tuple((c0, min(c0 + chunk, d_ff)) for c0 in range(0, d_ff, chunk))
    row = lambda i: (i, 0)
    grouped = lambda dil, cols: pl.BlockSpec((1, dil, tm // dil, cols), lambda i: (i // per_seq, 0, i % per_seq, 0))
    n_stats = 2 if merged else 1
    scratch = []
    for dil in dilations:
        if dil > 1:
            scratch.append(pltpu.VMEM((d // LANES, tm, LANES), F32))
    for _ in range(n_stats):
        for dil in dilations:
            if dil > 1:
                scratch.append(pltpu.VMEM((1, tm, STAT_LANES), F32))
    stats = (list(maxes) if merged else []) + list(dens)
    return pl.pallas_call(
        functools.partial(_post_kernel, dilations=dilations, merged=merged, final=final, d_ff=d_ff, chunks=chunks),
        grid=(m // tm,),
        in_specs=[grouped(dil, d) for dil in dilations]
                 + [grouped(dil, STAT_LANES) for _ in range(n_stats) for dil in dilations]
                 + [_resident((2 * STAT_LANES, d)),
                    pl.BlockSpec((tm, d), row),
                    pl.BlockSpec((1, 6, d), lambda i: (i // per_seq, 0, 0)),
                    _resident((d, d)),
                    _resident((1, d)),
                    _resident((d, 2 * d_ff)),
                    _resident((d_ff, d)),
                    _resident((1, d))],
        out_specs=pl.BlockSpec((tm, d), row),
        out_shape=jax.ShapeDtypeStruct((m, d), F32),
        scratch_shapes=scratch,
        compiler_params=_cparams(1),
        name="post_attn_ffn",
    )(*outs, *stats, expand2, x, mod, w_o, g.reshape(1, d), w_in, w_out, g_final.reshape(1, d))


def _t5_bucket(rel):
    half = T5_BUCKETS // 2
    max_exact = half // 2
    ret = jnp.where(rel > 0, half, 0)
    n = jnp.abs(rel)
    nf = jnp.maximum(n, 1).astype(F32)
    large = max_exact + (jnp.log(nf / max_exact) / math.log(T5_MAX_DISTANCE / max_exact)
                         * (half - max_exact)).astype(jnp.int32)
    large = jnp.minimum(large, half - 1)
    return ret + jnp.where(n < max_exact, n, large)


def _t5_table_kernel(tab_ref, idx_ref, o_ref, *, permute_heads):
    idx = idx_ref[0]
    hits = [idx == b for b in range(T5_BUCKETS)]
    for p in range(N_HEADS):
        head = (p % HEADS_PER_BLOCK) * N_KV_HEADS + p // HEADS_PER_BLOCK if permute_heads else p
        acc = jnp.full(idx.shape, NEG_INF, F32)
        for b in range(T5_BUCKETS):
            acc = jnp.where(hits[b], tab_ref[b * N_HEADS + head] * LOG2_E, acc)
        o_ref[0, p] = acc


def _t5_table(rel_bias, radius, dilation, width, permute_heads=False):
    offs = jnp.arange(3)[:, None, None] * radius
    rel = jnp.arange(width)[None, None, :] - offs - jnp.arange(Q_BLOCK)[None, :, None]
    idx = jnp.where(jnp.abs(rel) <= radius, _t5_bucket(rel * dilation), -1).astype(jnp.int32)
    return pl.pallas_call(
        functools.partial(_t5_table_kernel, permute_heads=permute_heads),
        grid=(3,),
        in_specs=[pl.BlockSpec(memory_space=pltpu.SMEM),
                  pl.BlockSpec((1, Q_BLOCK, width), lambda v: (v, 0, 0))],
        out_specs=pl.BlockSpec((1, N_HEADS, Q_BLOCK, width), lambda v: (v, 0, 0, 0)),
        out_shape=jax.ShapeDtypeStruct((3, N_HEADS, Q_BLOCK, width), F32),
        compiler_params=_cparams(1),
        name="t5_table",
    )(rel_bias.reshape(-1), idx)


def _na_table_kernel(rpb_ref, o_ref):
    head = pl.program_id(0)
    shape = (GRID_W, 2 * GRID_W)
    w = lax.broadcasted_iota(jnp.int32, shape, 0)
    lane = lax.broadcasted_iota(jnp.int32, shape, 1)
    c = lax.bitwise_and(lane, GRID_W - 1)
    half = lax.shift_right_logical(lane, int(math.log2(GRID_W)))
    col_start = jnp.clip(w - NA_COLS // 2, 0, GRID_W - NA_COLS)
    valid = (c >= col_start) & (c < col_start + NA_COLS)
    col_off = jnp.where(valid, c - w + (NA_COLS - 1), -1)
    hits = [col_off == cv for cv in range(NA_COL_OFFS)]
    per_row_off = []
    for ro in range(NA_ROW_OFFS):
        acc = jnp.full(shape, NEG_INF, F32)
        for cv in range(NA_COL_OFFS):
            acc = jnp.where(hits[cv], rpb_ref[(head * NA_ROW_OFFS + ro) * NA_COL_OFFS + cv] * LOG2_E, acc)
        per_row_off.append(acc)
    for t in range(NA_ROW_OFFS - 1):
        o_ref[0, t] = jnp.where(half == 0, per_row_off[t], per_row_off[t + 1])


def _na_table(rpb):
    n_t = NA_ROW_OFFS - 1
    return pl.pallas_call(
        _na_table_kernel,
        grid=(N_HEADS,),
        in_specs=[pl.BlockSpec(memory_space=pltpu.SMEM)],
        out_specs=pl.BlockSpec((1, n_t, GRID_W, 2 * GRID_W), lambda h: (h, 0, 0, 0)),
        out_shape=jax.ShapeDtypeStruct((N_HEADS, n_t, GRID_W, 2 * GRID_W), F32),
        compiler_params=_cparams(1),
        name="na_table",
    )(rpb.reshape(-1))


def _attend(q, k, v, rows, bias_fn, sink_fn):
    head_of_lane = lax.shift_right_logical(lax.broadcasted_iota(jnp.int32, (rows, LANE_BLOCK), 1),
                                           int(math.log2(HEAD_DIM)))
    stacked = jnp.concatenate(
        [q * jnp.where(head_of_lane == j, 1.0, 0.0).astype(BF16) for j in range(HEADS_PER_BLOCK)], axis=0)
    s = lax.dot_general(stacked, k, (((1,), (1,)), ((), ())), preferred_element_type=F32)
    ps, maxes, dens = [], [], []
    for j in range(HEADS_PER_BLOCK):
        sj = s[j * rows:(j + 1) * rows] + bias_fn(j)
        m = jnp.max(sj, axis=-1, keepdims=True)
        if sink_fn is not None:
            m = jnp.maximum(m, sink_fn(j))
        p = jnp.exp2(sj - m)
        l = jnp.sum(p, axis=-1, keepdims=True)
        if sink_fn is not None:
            l = l + jnp.exp2(sink_fn(j) - m)
        ps.append(p.astype(BF16))
        maxes.append(m)
        dens.append(l)
    pv = jnp.dot(jnp.concatenate(ps, axis=0), v, preferred_element_type=F32)
    out = pv[0:rows]
    for j in range(1, HEADS_PER_BLOCK):
        out = jnp.where(head_of_lane == j, pv[j * rows:(j + 1) * rows], out)
    return out, maxes, dens


def _stat_tile(cols, tile):
    lane = lax.broadcasted_iota(jnp.int32, tile, 1)
    out = jnp.zeros(tile, F32)
    for col, val in cols:
        out = jnp.where(lane == col, val, out)
    return out


def _band_attn_kernel(*refs, n_blk, radius, width, has_sink, want_max):
    if has_sink:
        sink_ref, refs = refs[0], refs[1:]
    q_ref, k_ref, v_ref, tab_ref, o_ref = refs[:5]
    stat_refs = refs[5:]
    n_seqs, seq = q_ref.shape[0], q_ref.shape[1]
    n_steps = seq // Q_BLOCK
    blk0 = pl.program_id(2) * n_blk
    accumulate = n_blk < N_HEAD_BLOCKS

    if accumulate:
        @pl.when(pl.program_id(2) == 0)
        def _():
            for ref in stat_refs:
                ref[...] = jnp.zeros(ref.shape, F32)

    for t in range(n_seqs * n_steps):
        g, n = t // n_steps, t % n_steps
        q0 = n * Q_BLOCK
        k0 = min(max(n * Q_BLOCK - radius, 0), seq - width)
        variant = 0 if n == 0 else (2 if n == n_steps - 1 else 1)
        max_cols, den_cols = [], []
        for b in range(n_blk):
            lanes = slice(b * LANE_BLOCK, (b + 1) * LANE_BLOCK)
            klanes = lanes if k_ref.shape[2] > LANE_BLOCK else slice(0, LANE_BLOCK)
            q = q_ref[g, pl.ds(q0, Q_BLOCK), lanes]
            k = k_ref[g, pl.ds(k0, width), klanes]
            v = v_ref[g, pl.ds(k0, width), klanes]
            bias_fn = lambda j: tab_ref[variant, b * HEADS_PER_BLOCK + j]
            sink_fn = None
            if has_sink:
                sink_fn = lambda j: sink_ref[j * N_KV_HEADS + blk0 + b] * LOG2_E
            out, maxes, dens = _attend(q, k, v, Q_BLOCK, bias_fn, sink_fn)
            o_ref[g, pl.ds(q0, Q_BLOCK), lanes] = out.astype(o_ref.dtype)
            for j in range(HEADS_PER_BLOCK):
                col = (blk0 + b) * HEADS_PER_BLOCK + j
                max_cols.append((col, maxes[j]))
                den_cols.append((col, dens[j]))
        for ref, cols in zip(stat_refs, ([max_cols] if want_max else []) + [den_cols]):
            tile = _stat_tile(cols, (Q_BLOCK, STAT_LANES))
            if accumulate:
                tile = tile + ref[g, pl.ds(q0, Q_BLOCK), :]
            ref[g, pl.ds(q0, Q_BLOCK), :] = tile


def _dilated_attention(qkv, table, dilation):
    batch, _, sub, n_cols = qkv.shape
    d_model = n_cols // 3
    n_seq = batch * dilation
    n_blk = 1 if dilation == 1 else N_HEAD_BLOCKS
    wide = n_blk * LANE_BLOCK
    stride = d_model // wide
    view = qkv.reshape(n_seq, sub, n_cols)
    n_steps = sub // Q_BLOCK
    per_step = max(1, ATTN_UNITS_IN_FLIGHT // (n_steps * n_blk))

    def col(part):
        return lambda s, _, h: (s, 0, part * stride + h)

    stat_spec = pl.BlockSpec((per_step, sub, STAT_LANES), lambda s, _, h: (s, 0, 0))
    stat_shape = jax.ShapeDtypeStruct((n_seq, sub, STAT_LANES), F32)
    o, mx, den = pl.pallas_call(
        functools.partial(_band_attn_kernel, n_blk=n_blk, radius=HEAD_DIM, width=2 * Q_BLOCK,
                          has_sink=False, want_max=True),
        grid=(n_seq // per_step, 1, N_HEAD_BLOCKS // n_blk),
        in_specs=[pl.BlockSpec((per_step, sub, wide), col(0)),
                  pl.BlockSpec((per_step, sub, wide), col(1)),
                  pl.BlockSpec((per_step, sub, wide), col(2)),
                  pl.BlockSpec((3, n_blk * HEADS_PER_BLOCK, Q_BLOCK, 2 * Q_BLOCK), lambda s, _, h: (0, h, 0, 0))],
        out_specs=[pl.BlockSpec((per_step, sub, wide), lambda s, _, h: (s, 0, h)), stat_spec, stat_spec],
        out_shape=[jax.ShapeDtypeStruct((n_seq, sub, d_model), BF16), stat_shape, stat_shape],
        compiler_params=_cparams(3),
        name=f"dilated_attn_d{dilation}",
    )(view, view, view, table)
    grouped = lambda a: a.reshape(batch, dilation, sub, a.shape[-1])
    return grouped(o), grouped(mx), grouped(den)


def _window_gqa_attention(qkv, table, sink, batch, seq):
    d_model = N_HEADS * HEAD_DIM
    view = qkv.reshape(batch, seq, qkv.shape[-1])
    k_blk = d_model // LANE_BLOCK
    o, den = pl.pallas_call(
        functools.partial(_band_attn_kernel, n_blk=1, radius=SWA_RADIUS, width=3 * Q_BLOCK,
                          has_sink=True, want_max=False),
        grid=(batch, 1, N_HEAD_BLOCKS),
        in_specs=[pl.BlockSpec(memory_space=pltpu.SMEM),
                  pl.BlockSpec((1, seq, LANE_BLOCK), lambda b, r, h: (b, 0, h)),
                  pl.BlockSpec((1, seq, LANE_BLOCK), lambda b, r, h: (b, 0, k_blk)),
                  pl.BlockSpec((1, seq, LANE_BLOCK), lambda b, r, h: (b, 0, k_blk + 1)),
                  pl.BlockSpec((3, HEADS_PER_BLOCK, Q_BLOCK, 3 * Q_BLOCK), lambda b, r, h: (0, h, 0, 0))],
        out_specs=[pl.BlockSpec((1, seq, LANE_BLOCK), lambda b, r, h: (b, 0, h)),
                   pl.BlockSpec((1, seq, STAT_LANES), lambda b, r, h: (b, 0, 0))],
        out_shape=[jax.ShapeDtypeStruct((batch, seq, d_model), BF16),
                   jax.ShapeDtypeStruct((batch, seq, STAT_LANES), F32)],
        compiler_params=_cparams(3),
        name="window_gqa_attn",
    )(sink, view, view, view, table)
    return o.reshape(batch, 1, seq, d_model), den.reshape(batch, 1, seq, STAT_LANES)


def _nbr_attn_kernel(q_ref, k_ref, v_ref, tab_ref, o_ref, den_ref):
    n_rows = q_ref.shape[1] // GRID_W
    n_keys = NA_ROWS * GRID_W
    blk = pl.program_id(1)

    @pl.when(blk == 0)
    def _():
        den_ref[...] = jnp.zeros(den_ref.shape, F32)

    for r in range(n_rows):
        row_start = min(max(r - NA_ROWS // 2, 0), n_rows - NA_ROWS)
        t0 = row_start - r + (NA_ROWS - 1)
        q0 = r * GRID_W
        k0 = row_start * GRID_W
        q = q_ref[0, pl.ds(q0, GRID_W), :]
        k = k_ref[0, pl.ds(k0, n_keys), :]
        v = v_ref[0, pl.ds(k0, n_keys), :]
        bias_fn = lambda j: jnp.concatenate([tab_ref[j, t0 + 2 * pr] for pr in range(NA_ROWS // 2)], axis=1)
        out, _, dens = _attend(q, k, v, GRID_W, bias_fn, None)
        o_ref[0, pl.ds(q0, GRID_W), :] = out.astype(o_ref.dtype)
        cols = [(blk * HEADS_PER_BLOCK + j, dens[j]) for j in range(HEADS_PER_BLOCK)]
        den_ref[0, pl.ds(q0, GRID_W), :] = den_ref[0, pl.ds(q0, GRID_W), :] + _stat_tile(cols, (GRID_W, STAT_LANES))


def _neighbourhood_attention(qkv, table, batch, seq):
    d_model = N_HEADS * HEAD_DIM
    view = qkv.reshape(batch, seq, 3 * d_model)
    nb = N_HEAD_BLOCKS
    blk = lambda part: pl.BlockSpec((1, seq, LANE_BLOCK), lambda b, h: (b, 0, part * nb + h))
    o, den = pl.pallas_call(
        _nbr_attn_kernel,
        grid=(batch, nb),
        in_specs=[blk(0), blk(1), blk(2),
                  pl.BlockSpec((HEADS_PER_BLOCK, NA_ROW_OFFS - 1, GRID_W, 2 * GRID_W), lambda b, h: (h, 0, 0, 0))],
        out_specs=[pl.BlockSpec((1, seq, LANE_BLOCK), lambda b, h: (b, 0, h)),
                   pl.BlockSpec((1, seq, STAT_LANES), lambda b, h: (b, 0, 0))],
        out_shape=[jax.ShapeDtypeStruct((batch, seq, d_model), BF16),
                   jax.ShapeDtypeStruct((batch, seq, STAT_LANES), F32)],
        compiler_params=_cparams(2),
        name="neighbourhood_attn",
    )(view, view, view, table)
    return o.reshape(batch, 1, seq, d_model), den.reshape(batch, 1, seq, STAT_LANES)


def kernel(x, c, rel_bias, ada_w, ada_b, norm_mix, norm_ffn, norm_final, a_w_in, a_w_out, b_w_in, b_w_out, b_rpb, c_w_in, c_w_out, c_sink, ffn_w_in, ffn_w_out):
    batch, seq, d_model = x.shape
    depth = ada_w.shape[0]
    assert d_model == N_HEADS * HEAD_DIM and seq % (16 * 2 * Q_BLOCK) == 0 and seq // GRID_W >= NA_ROWS

    mods = _ada(c, ada_w, ada_b).reshape(depth, batch, 6, d_model)
    xf = x.reshape(batch * seq, d_model)

    dils = tuple(dil for _, dil in DILATED_GROUPS)
    tabs_a = [_t5_table(rel_bias, HEAD_DIM, dil, 2 * Q_BLOCK) for dil in dils]
    tab_c = _t5_table(rel_bias, SWA_RADIUS, 1, 3 * Q_BLOCK, permute_heads=True)
    perm = jnp.array([(p % HEADS_PER_BLOCK) * N_KV_HEADS + p // HEADS_PER_BLOCK for p in range(N_HEADS)])

    for i in range(depth):
        kind, j = i % 3, i // 3
        mod = mods[i]
        if kind == 0:
            n_in = a_w_in.shape[2]
            scale = _q_col_scale(n_in, [(g * 3 * d_model, g * 3 * d_model + d_model) for g in range(len(dils))])
            qkvs = _proj(xf, norm_mix[i], mod, a_w_in[j].astype(BF16), scale, batch, seq, dils)
            outs, maxes, dens = zip(*[_dilated_attention(qkv, tab, dil) for qkv, tab, dil in zip(qkvs, tabs_a, dils)])
            post_dils, w_o = dils, a_w_out[j].astype(BF16)
        elif kind == 1:
            scale = _q_col_scale(3 * d_model, [(0, d_model)])
            [qkv] = _proj(xf, norm_mix[i], mod, b_w_in[j].astype(BF16), scale, batch, seq)
            o, den = _neighbourhood_attention(qkv, _na_table(b_rpb[j]), batch, seq)
            outs, maxes, dens, post_dils, w_o = [o], None, [den], (1,), b_w_out[j].astype(BF16)
        else:
            w_q = c_w_in[j][:, :d_model].reshape(d_model, N_HEADS, HEAD_DIM)[:, perm].reshape(d_model, d_model)
            w_in = jnp.concatenate([w_q, c_w_in[j][:, d_model:]], axis=1).astype(BF16)
            w_o = c_w_out[j].reshape(N_HEADS, HEAD_DIM, d_model)[perm].reshape(d_model, d_model).astype(BF16)
            scale = _q_col_scale(w_in.shape[1], [(0, d_model)])
            [qkv] = _proj(xf, norm_mix[i], mod, w_in, scale, batch, seq)
            o, den = _window_gqa_attention(qkv, tab_c, c_sink[j], batch, seq)
            outs, maxes, dens, post_dils = [o], None, [den], (1,)
        xf = _post(outs, maxes, dens, post_dils, xf, mod, w_o, norm_ffn[i], ffn_w_in[i].astype(BF16),
                   ffn_w_out[i].astype(BF16), norm_final, seq, final=(i == depth - 1))
    return xf.reshape(batch, seq, d_model)
```

```python
import functools
import math

import jax
import jax.numpy as jnp
from jax import lax
from jax.experimental import pallas as pl
from jax.experimental.pallas import tpu as pltpu

F32 = jnp.float32
BF16 = jnp.bfloat16

HEAD_DIM = 64
N_HEADS = 16
LANES = 128
SUBLANES = 8
HEADS_PER_BLOCK = 4
LANE_BLOCK = HEADS_PER_BLOCK * HEAD_DIM
N_HEAD_BLOCKS = N_HEADS // HEADS_PER_BLOCK
DILATED_GROUPS = ((128, 1), (512, 4), (2048, 16))
GRID_W = 64
NA_ROWS = 8
NA_COLS = 16
NA_ROW_OFFS = 2 * NA_ROWS - 1
NA_COL_OFFS = 2 * NA_COLS - 1
SWA_RADIUS = 128
N_KV_HEADS = 4
T5_BUCKETS = 32
T5_MAX_DISTANCE = 1024
EPS = 1e-6
NEG_INF = -1e30
LOG2_E = math.log2(math.e)
Q_SCALE = LOG2_E / math.sqrt(HEAD_DIM)
STAT_LANES = LANES
Q_BLOCK = 128
ATTN_UNITS_IN_FLIGHT = 64
VMEM_LIMIT = 58 * 1024 * 1024
PROJ_VMEM_BUDGET = 54 * 1024 * 1024


def _cparams(n_axes, vmem=VMEM_LIMIT):
    return pltpu.CompilerParams(dimension_semantics=("arbitrary",) * n_axes, vmem_limit_bytes=vmem)


def _resident(shape):
    return pl.BlockSpec(shape, lambda *_: (0,) * len(shape), pipeline_mode=pl.Buffered(1))


def _ada_kernel(c_ref, w_ref, b_ref, o_ref):
    c = c_ref[...]
    cond = c * jax.nn.sigmoid(c)
    o_ref[0] = jnp.dot(cond.astype(BF16), w_ref[0].astype(BF16), preferred_element_type=F32) + b_ref[0]


def _ada(c, ada_w, ada_b, tn=1536):
    depth, d, n = ada_w.shape
    b = c.shape[0]
    return pl.pallas_call(
        _ada_kernel,
        grid=(depth, n // tn),
        in_specs=[pl.BlockSpec((b, d), lambda l, j: (0, 0)),
                  pl.BlockSpec((1, d, tn), lambda l, j: (l, 0, j)),
                  pl.BlockSpec((1, 1, tn), lambda l, j: (l, 0, j))],
        out_specs=pl.BlockSpec((1, b, tn), lambda l, j: (l, 0, j)),
        out_shape=jax.ShapeDtypeStruct((depth, b, n), F32),
        compiler_params=_cparams(2),
        name="ada_mod",
    )(c, ada_w, ada_b.reshape(depth, 1, n))


def _modulated_norm(x, g, mod, shift_row, scale_row):
    ms = jnp.mean(x * x, axis=-1, keepdims=True)
    y = x * lax.rsqrt(ms + EPS) * g
    return y * (1.0 + mod[scale_row:scale_row + 1]) + mod[shift_row:shift_row + 1]


def _proj_kernel(x_ref, g_ref, mod_ref, w_ref, cs_ref, *rest, dilations, n_chunk):
    out_refs = rest[:len(dilations)]
    y = _modulated_norm(x_ref[...], g_ref[...], mod_ref[0], 0, 1)
    tm = y.shape[0]
    n_lane_tiles = y.shape[1] // LANES
    if max(dilations) > 1:
        hn_ref = rest[len(dilations)]
        for t in range(n_lane_tiles):
            hn_ref[t] = y[:, t * LANES:(t + 1) * LANES]
    n_group = w_ref.shape[1] // len(dilations)
    for grp, dil in enumerate(dilations):
        rows = tm // dil
        if dil == 1:
            h = y.astype(BF16)
        else:
            h = jnp.concatenate(
                [jnp.concatenate([hn_ref[t, pl.ds(r, rows, stride=dil), :] for t in range(n_lane_tiles)], axis=1)
                 for r in range(dil)], axis=0).astype(BF16)
        for c0 in range(0, n_group, n_chunk):
            w0 = grp * n_group + c0
            res = jnp.dot(h, w_ref[:, w0:w0 + n_chunk], preferred_element_type=F32)
            res = (res * cs_ref[:, w0:w0 + n_chunk]).astype(BF16)
            for r in range(dil):
                out_refs[grp][0, r, :, c0:c0 + n_chunk] = res[r * rows:(r + 1) * rows]


def _proj_rows(d, n, dilated, n_chunk):
    for tm in (1024, 512, 256):
        need = d * n * 2 + 2 * tm * (n * 2 + d * 4) + tm * d * 4 * (3 if dilated else 2) + 3 * tm * n_chunk * 4
        if need <= PROJ_VMEM_BUDGET:
            return tm
    return 128


def _proj(x, g, mod, w, col_scale, batch, seq, dilations=(1,), *, n_chunk=512):
    m, d = x.shape
    tm = _proj_rows(d, w.shape[1], max(dilations) > 1, n_chunk)
    n_group = w.shape[1] // len(dilations)
    per_seq = seq // tm
    scratch = [pltpu.VMEM((d // LANES, tm, LANES), F32)] if max(dilations) > 1 else []
    return pl.pallas_call(
        functools.partial(_proj_kernel, dilations=dilations, n_chunk=n_chunk),
        grid=(m // tm,),
        in_specs=[pl.BlockSpec((tm, d), lambda i: (i, 0)),
                  _resident((1, d)),
                  pl.BlockSpec((1, 6, d), lambda i: (i // per_seq, 0, 0)),
                  _resident(w.shape),
                  _resident((1, w.shape[1]))],
        out_specs=[pl.BlockSpec((1, dil, tm // dil, n_group), lambda i: (i // per_seq, 0, i % per_seq, 0))
                   for dil in dilations],
        out_shape=[jax.ShapeDtypeStruct((batch, dil, seq // dil, n_group), BF16) for dil in dilations],
        scratch_shapes=scratch,
        compiler_params=_cparams(1),
        name="norm_proj",
    )(x, g.reshape(1, d), mod, w, col_scale.reshape(1, -1))


def _q_col_scale(n_cols, q_ranges):
    scale = jnp.ones((n_cols,), F32)
    for c0, c1 in q_ranges:
        scale = scale.at[c0:c1].set(Q_SCALE)
    return scale


def _row_pitch(dil):
    return dil + SUBLANES if dil % (2 * SUBLANES) == 0 else dil


def _post_kernel(*refs, dilations, merged, final, d_ff, chunks):
    n = len(dilations)
    o_refs = refs[:n]
    m_refs = refs[n:2 * n] if merged else ()
    l_refs = refs[2 * n:3 * n] if merged else refs[n:2 * n]
    k = 3 * n if merged else 2 * n
    e_ref, x_ref, mod_ref, wo_ref, g_ref, win_ref, wout_ref, gf_ref, out_ref = refs[k:k + 9]
    scratch = list(refs[k + 9:])
    tm = x_ref.shape[0]

    def natural_order(ref, dil):
        if dil == 1:
            return ref[0, 0].astype(F32)
        scr_ref = scratch.pop(0)
        n_lane_tiles = scr_ref.shape[0]
        pitch = _row_pitch(dil)
        for r in range(dil):
            part = ref[0, r].astype(F32)
            for t in range(n_lane_tiles):
                scr_ref[t, pl.ds(r, tm // dil, stride=pitch), :] = part[:, t * LANES:(t + 1) * LANES]
        if pitch == dil:
            return jnp.concatenate([scr_ref[t] for t in range(n_lane_tiles)], axis=1)
        return jnp.concatenate(
            [jnp.concatenate([scr_ref[t, l * pitch:l * pitch + dil, :] for l in range(tm // dil)], axis=0)
             for t in range(n_lane_tiles)], axis=1)

    def expand(wgt):
        hi = wgt.astype(BF16)
        lo = (wgt - hi.astype(F32)).astype(BF16)
        return jnp.dot(jnp.concatenate([hi, lo], axis=1), e_ref[...], preferred_element_type=F32)

    is_head = lax.broadcasted_iota(jnp.int32, (tm, STAT_LANES), 1) < N_HEADS
    outs = [natural_order(r, dil) for r, dil in zip(o_refs, dilations)]
    dens = [jnp.where(is_head, natural_order(r, dil), 1.0) for r, dil in zip(l_refs, dilations)]
    if merged:
        maxes = [natural_order(r, dil) for r, dil in zip(m_refs, dilations)]
        top = functools.reduce(jnp.maximum, maxes)
        es = [jnp.exp2(mg - top) for mg in maxes]
        inv = 1.0 / functools.reduce(jnp.add, [e * l for e, l in zip(es, dens)])
        att = functools.reduce(jnp.add, [expand(jnp.where(is_head, e * inv, 0.0)) * o for e, o in zip(es, outs)])
    else:
        att = expand(jnp.where(is_head, 1.0 / dens[0], 0.0)) * outs[0]

    mod = mod_ref[0]
    x = x_ref[...] + mod[2:3] * jnp.dot(att.astype(BF16), wo_ref[...], preferred_element_type=F32)

    h = _modulated_norm(x, g_ref[...], mod, 3, 4).astype(BF16)
    acc = None
    for c0, c1 in chunks:
        gate = jnp.dot(h, win_ref[:, c0:c1], preferred_element_type=F32)
        up = jnp.dot(h, win_ref[:, d_ff + c0:d_ff + c1], preferred_element_type=F32)
        act = (gate * jax.nn.sigmoid(gate) * up).astype(BF16)
        part = jnp.dot(act, wout_ref[c0:c1, :], preferred_element_type=F32)
        acc = part if acc is None else acc + part
    y = x + mod[5:6] * acc
    if final:
        ms = jnp.mean(y * y, axis=-1, keepdims=True)
        y = y * lax.rsqrt(ms + EPS) * gf_ref[...]
    out_ref[...] = y


def _post(outs, maxes, dens, dilations, x, mod, w_o, g, w_in, w_out, g_final, seq, *, final, tm=512, chunk=512):
    m, d = x.shape
    d_ff = w_out.shape[0]
    per_seq = seq // tm
    merged = maxes is not None
    head_of_lane = jnp.arange(d) // HEAD_DIM
    expand = (jnp.arange(STAT_LANES)[:, None] == head_of_lane[None, :]).astype(BF16)
    expand2 = jnp.concatenate([expand, expand], axis=0)
    chunks = tuple((c0, min(c0 + chunk, d_ff)) for c0 in range(0, d_ff, chunk))
    row = lambda i: (i, 0)
    grouped = lambda dil, cols: pl.BlockSpec((1, dil, tm // dil, cols), lambda i: (i // per_seq, 0, i % per_seq, 0))
    n_stats = 2 if merged else 1
    scratch = []
    for dil in dilations:
        if dil > 1:
            scratch.append(pltpu.VMEM((d // LANES, tm // dil * _row_pitch(dil), LANES), F32))
    for _ in range(n_stats):
        for dil in dilations:
            if dil > 1:
                scratch.append(pltpu.VMEM((1, tm // dil * _row_pitch(dil), STAT_LANES), F32))
    stats = (list(maxes) if merged else []) + list(dens)
    return pl.pallas_call(
        functools.partial(_post_kernel, dilations=dilations, merged=merged, final=final, d_ff=d_ff, chunks=chunks),
        grid=(m // tm,),
        in_specs=[grouped(dil, d) for dil in dilations]
                 + [grouped(dil, STAT_LANES) for _ in range(n_stats) for dil in dilations]
                 + [_resident((2 * STAT_LANES, d)),
                    pl.BlockSpec((tm, d), row),
                    pl.BlockSpec((1, 6, d), lambda i: (i // per_seq, 0, 0)),
                    _resident((d, d)),
                    _resident((1, d)),
                    _resident((d, 2 * d_ff)),
                    _resident((d_ff, d)),
                    _resident((1, d))],
        out_specs=pl.BlockSpec((tm, d), row),
        out_shape=jax.ShapeDtypeStruct((m, d), F32),
        scratch_shapes=scratch,
        compiler_params=_cparams(1),
        name="post_attn_ffn",
    )(*outs, *stats, expand2, x, mod, w_o, g.reshape(1, d), w_in, w_out, g_final.reshape(1, d))


def _t5_bucket(rel):
    half = T5_BUCKETS // 2
    max_exact = half // 2
    ret = jnp.where(rel > 0, half, 0)
    n = jnp.abs(rel)
    nf = jnp.maximum(n, 1).astype(F32)
    large = max_exact + (jnp.log(nf / max_exact) / math.log(T5_MAX_DISTANCE / max_exact)
                         * (half - max_exact)).astype(jnp.int32)
    large = jnp.minimum(large, half - 1)
    return ret + jnp.where(n < max_exact, n, large)


def _t5_table_kernel(tab_ref, idx_ref, o_ref, *, permute_heads):
    idx = idx_ref[0]
    hits = [idx == b for b in range(T5_BUCKETS)]
    for p in range(N_HEADS):
        head = (p % HEADS_PER_BLOCK) * N_KV_HEADS + p // HEADS_PER_BLOCK if permute_heads else p
        acc = jnp.full(idx.shape, NEG_INF, F32)
        for b in range(T5_BUCKETS):
            acc = jnp.where(hits[b], tab_ref[b * N_HEADS + head] * LOG2_E, acc)
        o_ref[0, p] = acc


def _t5_table(rel_bias, radius, dilation, width, permute_heads=False):
    offs = jnp.arange(3)[:, None, None] * radius
    rel = jnp.arange(width)[None, None, :] - offs - jnp.arange(Q_BLOCK)[None, :, None]
    idx = jnp.where(jnp.abs(rel) <= radius, _t5_bucket(rel * dilation), -1).astype(jnp.int32)
    return pl.pallas_call(
        functools.partial(_t5_table_kernel, permute_heads=permute_heads),
        grid=(3,),
        in_specs=[pl.BlockSpec(memory_space=pltpu.SMEM),
                  pl.BlockSpec((1, Q_BLOCK, width), lambda v: (v, 0, 0))],
        out_specs=pl.BlockSpec((1, N_HEADS, Q_BLOCK, width), lambda v: (v, 0, 0, 0)),
        out_shape=jax.ShapeDtypeStruct((3, N_HEADS, Q_BLOCK, width), F32),
        compiler_params=_cparams(1),
        name="t5_table",
    )(rel_bias.reshape(-1), idx)


def _na_table_kernel(rpb_ref, o_ref):
    head = pl.program_id(0)
    shape = (GRID_W, 2 * GRID_W)
    w = lax.broadcasted_iota(jnp.int32, shape, 0)
    lane = lax.broadcasted_iota(jnp.int32, shape, 1)
    c = lax.bitwise_and(lane, GRID_W - 1)
    half = lax.shift_right_logical(lane, int(math.log2(GRID_W)))
    col_start = jnp.clip(w - NA_COLS // 2, 0, GRID_W - NA_COLS)
    valid = (c >= col_start) & (c < col_start + NA_COLS)
    col_off = jnp.where(valid, c - w + (NA_COLS - 1), -1)
    hits = [col_off == cv for cv in range(NA_COL_OFFS)]
    per_row_off = []
    for ro in range(NA_ROW_OFFS):
        acc = jnp.full(shape, NEG_INF, F32)
        for cv in range(NA_COL_OFFS):
            acc = jnp.where(hits[cv], rpb_ref[(head * NA_ROW_OFFS + ro) * NA_COL_OFFS + cv] * LOG2_E, acc)
        per_row_off.append(acc)
    for t in range(NA_ROW_OFFS - 1):
        o_ref[0, t] = jnp.where(half == 0, per_row_off[t], per_row_off[t + 1])


def _na_table(rpb):
    n_t = NA_ROW_OFFS - 1
    return pl.pallas_call(
        _na_table_kernel,
        grid=(N_HEADS,),
        in_specs=[pl.BlockSpec(memory_space=pltpu.SMEM)],
        out_specs=pl.BlockSpec((1, n_t, GRID_W, 2 * GRID_W), lambda h: (h, 0, 0, 0)),
        out_shape=jax.ShapeDtypeStruct((N_HEADS, n_t, GRID_W, 2 * GRID_W), F32),
        compiler_params=_cparams(1),
        name="na_table",
    )(rpb.reshape(-1))


def _attend(q, k, v, rows, bias_fn, sink_fn):
    head_of_lane = lax.shift_right_logical(lax.broadcasted_iota(jnp.int32, (rows, LANE_BLOCK), 1),
                                           int(math.log2(HEAD_DIM)))
    stacked = jnp.concatenate(
        [q * jnp.where(head_of_lane == j, 1.0, 0.0).astype(BF16) for j in range(HEADS_PER_BLOCK)], axis=0)
    s = lax.dot_general(stacked, k, (((1,), (1,)), ((), ())), preferred_element_type=F32)
    ps, maxes, dens = [], [], []
    for j in range(HEADS_PER_BLOCK):
        sj = s[j * rows:(j + 1) * rows] + bias_fn(j)
        m = jnp.max(sj, axis=-1, keepdims=True)
        if sink_fn is not None:
            m = jnp.maximum(m, sink_fn(j))
        p = jnp.exp2(sj - m)
        l = jnp.sum(p, axis=-1, keepdims=True)
        if sink_fn is not None:
            l = l + jnp.exp2(sink_fn(j) - m)
        ps.append(p.astype(BF16))
        maxes.append(m)
        dens.append(l)
    pv = jnp.dot(jnp.concatenate(ps, axis=0), v, preferred_element_type=F32)
    out = pv[0:rows]
    for j in range(1, HEADS_PER_BLOCK):
        out = jnp.where(head_of_lane == j, pv[j * rows:(j + 1) * rows], out)
    return out, maxes, dens


def _stat_tile(cols, tile):
    lane = lax.broadcasted_iota(jnp.int32, tile, 1)
    out = jnp.zeros(tile, F32)
    for col, val in cols:
        out = jnp.where(lane == col, val, out)
    return out


def _band_attn_kernel(*refs, n_blk, radius, width, has_sink, want_max):
    if has_sink:
        sink_ref, refs = refs[0], refs[1:]
    q_ref, k_ref, v_ref, tab_ref, o_ref = refs[:5]
    stat_refs = refs[5:]
    n_seqs, seq = q_ref.shape[0], q_ref.shape[1]
    n_steps = seq // Q_BLOCK
    blk0 = pl.program_id(2) * n_blk
    accumulate = n_blk < N_HEAD_BLOCKS

    if accumulate:
        @pl.when(pl.program_id(2) == 0)
        def _():
            for ref in stat_refs:
                ref[...] = jnp.zeros(ref.shape, F32)

    for t in range(n_seqs * n_steps):
        g, n = t // n_steps, t % n_steps
        q0 = n * Q_BLOCK
        k0 = min(max(n * Q_BLOCK - radius, 0), seq - width)
        variant = 0 if n == 0 else (2 if n == n_steps - 1 else 1)
        max_cols, den_cols = [], []
        for b in range(n_blk):
            lanes = slice(b * LANE_BLOCK, (b + 1) * LANE_BLOCK)
            klanes = lanes if k_ref.shape[2] > LANE_BLOCK else slice(0, LANE_BLOCK)
            q = q_ref[g, pl.ds(q0, Q_BLOCK), lanes]
            k = k_ref[g, pl.ds(k0, width), klanes]
            v = v_ref[g, pl.ds(k0, width), klanes]
            bias_fn = lambda j: tab_ref[variant, b * HEADS_PER_BLOCK + j]
            sink_fn = None
            if has_sink:
                sink_fn = lambda j: sink_ref[j * N_KV_HEADS + blk0 + b] * LOG2_E
            out, maxes, dens = _attend(q, k, v, Q_BLOCK, bias_fn, sink_fn)
            o_ref[g, pl.ds(q0, Q_BLOCK), lanes] = out.astype(o_ref.dtype)
            for j in range(HEADS_PER_BLOCK):
                col = (blk0 + b) * HEADS_PER_BLOCK + j
                max_cols.append((col, maxes[j]))
                den_cols.append((col, dens[j]))
        for ref, cols in zip(stat_refs, ([max_cols] if want_max else []) + [den_cols]):
            tile = _stat_tile(cols, (Q_BLOCK, STAT_LANES))
            if accumulate:
                tile = tile + ref[g, pl.ds(q0, Q_BLOCK), :]
            ref[g, pl.ds(q0, Q_BLOCK), :] = tile


def _dilated_attention(qkv, table, dilation):
    batch, _, sub, n_cols = qkv.shape
    d_model = n_cols // 3
    n_seq = batch * dilation
    n_blk = 1 if dilation == 1 else N_HEAD_BLOCKS
    wide = n_blk * LANE_BLOCK
    stride = d_model // wide
    view = qkv.reshape(n_seq, sub, n_cols)
    n_steps = sub // Q_BLOCK
    per_step = math.gcd(n_seq, max(1, ATTN_UNITS_IN_FLIGHT // (n_steps * n_blk)))

    def col(part):
        return lambda s, _, h: (s, 0, part * stride + h)

    stat_spec = pl.BlockSpec((per_step, sub, STAT_LANES), lambda s, _, h: (s, 0, 0))
    stat_shape = jax.ShapeDtypeStruct((n_seq, sub, STAT_LANES), F32)
    o, mx, den = pl.pallas_call(
        functools.partial(_band_attn_kernel, n_blk=n_blk, radius=HEAD_DIM, width=2 * Q_BLOCK,
                          has_sink=False, want_max=True),
        grid=(n_seq // per_step, 1, N_HEAD_BLOCKS // n_blk),
        in_specs=[pl.BlockSpec((per_step, sub, wide), col(0)),
                  pl.BlockSpec((per_step, sub, wide), col(1)),
                  pl.BlockSpec((per_step, sub, wide), col(2)),
                  pl.BlockSpec((3, n_blk * HEADS_PER_BLOCK, Q_BLOCK, 2 * Q_BLOCK), lambda s, _, h: (0, h, 0, 0))],
        out_specs=[pl.BlockSpec((per_step, sub, wide), lambda s, _, h: (s, 0, h)), stat_spec, stat_spec],
        out_shape=[jax.ShapeDtypeStruct((n_seq, sub, d_model), BF16), stat_shape, stat_shape],
        compiler_params=_cparams(3),
        name=f"dilated_attn_d{dilation}",
    )(view, view, view, table)
    grouped = lambda a: a.reshape(batch, dilation, sub, a.shape[-1])
    return grouped(o), grouped(mx), grouped(den)


def _window_gqa_attention(qkv, table, sink, batch, seq):
    d_model = N_HEADS * HEAD_DIM
    view = qkv.reshape(batch, seq, qkv.shape[-1])
    k_blk = d_model // LANE_BLOCK
    o, den = pl.pallas_call(
        functools.partial(_band_attn_kernel, n_blk=1, radius=SWA_RADIUS, width=3 * Q_BLOCK,
                          has_sink=True, want_max=False),
        grid=(batch, 1, N_HEAD_BLOCKS),
        in_specs=[pl.BlockSpec(memory_space=pltpu.SMEM),
                  pl.BlockSpec((1, seq, LANE_BLOCK), lambda b, r, h: (b, 0, h)),
                  pl.BlockSpec((1, seq, LANE_BLOCK), lambda b, r, h: (b, 0, k_blk)),
                  pl.BlockSpec((1, seq, LANE_BLOCK), lambda b, r, h: (b, 0, k_blk + 1)),
                  pl.BlockSpec((3, HEADS_PER_BLOCK, Q_BLOCK, 3 * Q_BLOCK), lambda b, r, h: (0, h, 0, 0))],
        out_specs=[pl.BlockSpec((1, seq, LANE_BLOCK), lambda b, r, h: (b, 0, h)),
                   pl.BlockSpec((1, seq, STAT_LANES), lambda b, r, h: (b, 0, 0))],
        out_shape=[jax.ShapeDtypeStruct((batch, seq, d_model), BF16),
                   jax.ShapeDtypeStruct((batch, seq, STAT_LANES), F32)],
        compiler_params=_cparams(3),
        name="window_gqa_attn",
    )(sink, view, view, view, table)
    return o.reshape(batch, 1, seq, d_model), den.reshape(batch, 1, seq, STAT_LANES)


def _nbr_attn_kernel(q_ref, k_ref, v_ref, tab_ref, o_ref, den_ref):
    n_rows = q_ref.shape[1] // GRID_W
    n_keys = NA_ROWS * GRID_W
    blk = pl.program_id(1)

    @pl.when(blk == 0)
    def _():
        den_ref[...] = jnp.zeros(den_ref.shape, F32)

    for r in range(n_rows):
        row_start = min(max(r - NA_ROWS // 2, 0), n_rows - NA_ROWS)
        t0 = row_start - r + (NA_ROWS - 1)
        q0 = r * GRID_W
        k0 = row_start * GRID_W
        q = q_ref[0, pl.ds(q0, GRID_W), :]
        k = k_ref[0, pl.ds(k0, n_keys), :]
        v = v_ref[0, pl.ds(k0, n_keys), :]
        bias_fn = lambda j: jnp.concatenate([tab_ref[j, t0 + 2 * pr] for pr in range(NA_ROWS // 2)], axis=1)
        out, _, dens = _attend(q, k, v, GRID_W, bias_fn, None)
        o_ref[0, pl.ds(q0, GRID_W), :] = out.astype(o_ref.dtype)
        cols = [(blk * HEADS_PER_BLOCK + j, dens[j]) for j in range(HEADS_PER_BLOCK)]
        den_ref[0, pl.ds(q0, GRID_W), :] = den_ref[0, pl.ds(q0, GRID_W), :] + _stat_tile(cols, (GRID_W, STAT_LANES))


def _neighbourhood_attention(qkv, table, batch, seq):
    d_model = N_HEADS * HEAD_DIM
    view = qkv.reshape(batch, seq, 3 * d_model)
    nb = N_HEAD_BLOCKS
    blk = lambda part: pl.BlockSpec((1, seq, LANE_BLOCK), lambda b, h: (b, 0, part * nb + h))
    o, den = pl.pallas_call(
        _nbr_attn_kernel,
        grid=(batch, nb),
        in_specs=[blk(0), blk(1), blk(2),
                  pl.BlockSpec((HEADS_PER_BLOCK, NA_ROW_OFFS - 1, GRID_W, 2 * GRID_W), lambda b, h: (h, 0, 0, 0))],
        out_specs=[pl.BlockSpec((1, seq, LANE_BLOCK), lambda b, h: (b, 0, h)),
                   pl.BlockSpec((1, seq, STAT_LANES), lambda b, h: (b, 0, 0))],
        out_shape=[jax.ShapeDtypeStruct((batch, seq, d_model), BF16),
                   jax.ShapeDtypeStruct((batch, seq, STAT_LANES), F32)],
        compiler_params=_cparams(2),
        name="neighbourhood_attn",
    )(view, view, view, table)
    return o.reshape(batch, 1, seq, d_model), den.reshape(batch, 1, seq, STAT_LANES)


def kernel(x, c, rel_bias, ada_w, ada_b, norm_mix, norm_ffn, norm_final, a_w_in, a_w_out, b_w_in, b_w_out, b_rpb, c_w_in, c_w_out, c_sink, ffn_w_in, ffn_w_out):
    batch, seq, d_model = x.shape
    depth = ada_w.shape[0]
    assert d_model == N_HEADS * HEAD_DIM and seq % (16 * 2 * Q_BLOCK) == 0 and seq // GRID_W >= NA_ROWS

    mods = _ada(c, ada_w, ada_b).reshape(depth, batch, 6, d_model)
    xf = x.reshape(batch * seq, d_model)

    dils = tuple(dil for _, dil in DILATED_GROUPS)
    tabs_a = [_t5_table(rel_bias, HEAD_DIM, dil, 2 * Q_BLOCK) for dil in dils]
    tab_c = _t5_table(rel_bias, SWA_RADIUS, 1, 3 * Q_BLOCK, permute_heads=True)
    perm = jnp.array([(p % HEADS_PER_BLOCK) * N_KV_HEADS + p // HEADS_PER_BLOCK for p in range(N_HEADS)])

    for i in range(depth):
        kind, j = i % 3, i // 3
        mod = mods[i]
        if kind == 0:
            n_in = a_w_in.shape[2]
            scale = _q_col_scale(n_in, [(g * 3 * d_model, g * 3 * d_model + d_model) for g in range(len(dils))])
            qkvs = _proj(xf, norm_mix[i], mod, a_w_in[j].astype(BF16), scale, batch, seq, dils)
            outs, maxes, dens = zip(*[_dilated_attention(qkv, tab, dil) for qkv, tab, dil in zip(qkvs, tabs_a, dils)])
            post_dils, w_o = dils, a_w_out[j].astype(BF16)
        elif kind == 1:
            scale = _q_col_scale(3 * d_model, [(0, d_model)])
            [qkv] = _proj(xf, norm_mix[i], mod, b_w_in[j].astype(BF16), scale, batch, seq)
            o, den = _neighbourhood_attention(qkv, _na_table(b_rpb[j]), batch, seq)
            outs, maxes, dens, post_dils, w_o = [o], None, [den], (1,), b_w_out[j].astype(BF16)
        else:
            w_q = c_w_in[j][:, :d_model].reshape(d_model, N_HEADS, HEAD_DIM)[:, perm].reshape(d_model, d_model)
            w_in = jnp.concatenate([w_q, c_w_in[j][:, d_model:]], axis=1).astype(BF16)
            w_o = c_w_out[j].reshape(N_HEADS, HEAD_DIM, d_model)[perm].reshape(d_model, d_model).astype(BF16)
            scale = _q_col_scale(w_in.shape[1], [(0, d_model)])
            [qkv] = _proj(xf, norm_mix[i], mod, w_in, scale, batch, seq)
            o, den = _window_gqa_attention(qkv, tab_c, c_sink[j], batch, seq)
            outs, maxes, dens, post_dils = [o], None, [den], (1,)
        xf = _post(outs, maxes, dens, post_dils, xf, mod, w_o, norm_ffn[i], ffn_w_in[i].astype(BF16),
                   ffn_w_out[i].astype(BF16), norm_final, seq, final=(i == depth - 1))
    return xf.reshape(batch, seq, d_model)
```
